```python
import math
import jax, jax.numpy as jnp
from jax import lax
import numpy as np

D_MODEL = 1024
BATCH = 2
SEQ = 8192
DEPTH = 1

GRID_W = 64
CTX_LEN = 256
N_HEADS_M = 4
D_MLSTM = 1024
HEAD_DIM_M = D_MLSTM // N_HEADS_M
MLSTM_CHUNK = 128
CONV_W = 3
N_GROUPS_S = 4
D_SGU = 1024
GROUP_DIM_S = D_SGU // N_GROUPS_S
SGU_CHUNK = 128
D_FF = 2816
N_MOD = 9
POS_BASE = 10000.0
F_BIAS_LO = 3.0
F_BIAS_HI = 6.0
EPS = 1e-6
PROJ_SIZES = (D_MLSTM, D_MLSTM, D_MLSTM, 4 * N_HEADS_M, D_MLSTM, D_SGU, D_SGU, D_MODEL, D_MODEL)
N_STATE_PIECES = 4
D_PROJ = sum(PROJ_SIZES)

kernel_name = 'hybrid_mlstm_sgu_macaron_block'


def rmsnorm(x, g):
    x32 = x.astype(jnp.float32)
    y = x32 * lax.rsqrt(jnp.mean(x32 * x32, axis=-1, keepdims=True) + EPS)
    return (y * g.astype(jnp.float32)).astype(x.dtype)


def modulate(x, shift, scale):
    return x * (1.0 + scale[:, None, :]) + shift[:, None, :]


def ffn_sublayer(h, shift, scale, gate, g, w_in, w_out):
    hn = modulate(rmsnorm(h, g), shift, scale)
    a, b = jnp.split(hn @ w_in, 2, axis=-1)
    y = (jax.nn.silu(a) * b) @ w_out
    return h + 0.5 * gate[:, None, :] * y


def grid_pos_emb(rows):
    t = jnp.arange(rows * GRID_W)
    r = (t // GRID_W).astype(jnp.float32)
    col = (t % GRID_W).astype(jnp.float32)
    quarter = D_MODEL // 4
    freqs = jnp.exp(-math.log(POS_BASE) * jnp.arange(quarter, dtype=jnp.float32) / quarter)
    ar = r[:, None] * freqs
    ac = col[:, None] * freqs
    return jnp.concatenate([jnp.sin(ar), jnp.cos(ar), jnp.sin(ac), jnp.cos(ac)], axis=-1)


def split_cols(p, sizes):
    out = []
    off = 0
    for s in sizes:
        out.append(p[..., off:off + s])
        off += s
    return out


def project(h, shift, scale, g, w_in, sizes):
    hn = modulate(rmsnorm(h, g), shift, scale)
    return split_cols(hn @ w_in[:, :sum(sizes)], sizes)


def short_conv(x, w, b):
    pad = CONV_W // 2
    s = x.shape[1]
    xp = jnp.pad(x, ((0, 0), (pad, pad), (0, 0)))
    return sum(xp[:, j:j + s] * w[j] for j in range(CONV_W)) + b


def to_heads(t):
    b, s, _ = t.shape
    return t.reshape(b, s, N_HEADS_M, HEAD_DIM_M).transpose(0, 2, 1, 3).astype(jnp.float32)


def mlstm_prep(q, k, v, gates, conv_w, conv_b, b_gates):
    qk = jax.nn.silu(short_conv(jnp.concatenate([q, k], axis=-1), conv_w, conv_b))
    q, k = jnp.split(qk, 2, axis=-1)
    q = to_heads(q) * HEAD_DIM_M ** -0.5
    k = to_heads(k)
    v = to_heads(v)
    gt = jnp.moveaxis((gates + b_gates).astype(jnp.float32), -1, 1)
    nh = N_HEADS_M
    fwd = (gt[:, :nh], jax.nn.log_sigmoid(gt[:, nh:2 * nh]))
    bwd = (gt[:, 2 * nh:3 * nh], jax.nn.log_sigmoid(gt[:, 3 * nh:]))
    return q, k, v, fwd, bwd


def zero_state(bsz):
    return (jnp.zeros((bsz, N_HEADS_M, HEAD_DIM_M, HEAD_DIM_M), jnp.float32),
            jnp.zeros((bsz, N_HEADS_M, HEAD_DIM_M), jnp.float32),
            jnp.zeros((bsz, N_HEADS_M), jnp.float32))


def mlstm_scan(q, k, v, li, lf, state, reverse, emit):
    if reverse:
        q, k, v = jnp.flip(q, 2), jnp.flip(k, 2), jnp.flip(v, 2)
        li, lf = jnp.flip(li, 2), jnp.flip(lf, 2)
    bsz, nh, s, dh = q.shape
    nc = s // MLSTM_CHUNK

    def to_chunks(t):
        return jnp.moveaxis(t.reshape(t.shape[:2] + (nc, MLSTM_CHUNK) + t.shape[3:]), 2, 0)

    tri = jnp.tril(jnp.ones((MLSTM_CHUNK, MLSTM_CHUNK), dtype=bool))

    def body(carry, xs):
        c_st, n_st, m_st = carry
        qc, kc, vc, lic, lfc = xs
        b = jnp.cumsum(lfc, axis=-1)
        g = b[..., -1]
        w = g[..., None] - b + lic
        m_new = jnp.maximum(g + m_st, jnp.max(w, axis=-1))
        ws = jnp.exp(w - m_new[..., None])
        decay = jnp.exp(g + m_st - m_new)
        c_new = decay[..., None, None] * c_st + jnp.einsum('bhsv,bhsk->bhvk', vc * ws[..., None], kc)
        n_new = decay[..., None] * n_st + jnp.einsum('bhs,bhsk->bhk', ws, kc)
        if not emit:
            return (c_new, n_new, m_new), None
        dm = b[..., :, None] - b[..., None, :] + lic[..., None, :]
        dm = jnp.where(tri, dm, -jnp.inf)
        a = b + m_st[..., None]
        m_t = jnp.maximum(a, jnp.max(dm, axis=-1))
        s_ts = jnp.einsum('bhtk,bhsk->bhts', qc, kc) * jnp.exp(dm - m_t[..., None])
        inter = jnp.exp(a - m_t)
        num = jnp.einsum('bhts,bhsv->bhtv', s_ts, vc) + inter[..., None] * jnp.einsum('bhvk,bhtk->bhtv', c_st, qc)
        den = jnp.sum(s_ts, axis=-1) + inter * jnp.einsum('bhk,bhtk->bht', n_st, qc)
        h = num / jnp.maximum(jnp.abs(den), jnp.exp(-m_t))[..., None]
        return (c_new, n_new, m_new), h

    state, hs = lax.scan(body, state, (to_chunks(q), to_chunks(k), to_chunks(v), to_chunks(li), to_chunks(lf)))
    if not emit:
        return None, state
    h = jnp.moveaxis(hs, 0, 2).reshape(bsz, nh, s, dh)
    if reverse:
        h = jnp.flip(h, 2)
    return h, state


def head_layernorm(h, g):
    mu = jnp.mean(h, axis=-1, keepdims=True)
    hc = h - mu
    y = hc * lax.rsqrt(jnp.mean(hc * hc, axis=-1, keepdims=True) + EPS)
    bsz, nh, s, dh = h.shape
    return y.transpose(0, 2, 1, 3).reshape(bsz, s, nh * dh) * g.astype(jnp.float32)


def spatial_gating(u, v, g, w_s, b_s):
    bsz, s, _ = v.shape
    nc = s // SGU_CHUNK
    vn = rmsnorm(v, g).reshape(bsz, nc, SGU_CHUNK, N_GROUPS_S, GROUP_DIM_S)
    mixed = jnp.einsum('gts,bnsgc->bntgc', w_s, vn) + jnp.swapaxes(b_s, 0, 1)[:, :, None]
    return u * mixed.reshape(bsz, s, D_SGU)


def merge_branches(pieces, h_m, g_head, g_sgu, w_s, b_s, w_a, w_b, w_o):
    o, u, vs, ga, gb = pieces
    y_a = jax.nn.sigmoid(o) * head_layernorm(h_m, g_head).astype(o.dtype)
    y_b = spatial_gating(jax.nn.gelu(u), jax.nn.gelu(vs), g_sgu, w_s, b_s)
    mixed = jax.nn.sigmoid(ga) * (y_a @ w_a) + jax.nn.sigmoid(gb) * (y_b @ w_b)
    return mixed @ w_o


def setup_inputs(seed: int = 0) -> dict:
    key = jax.random.key(seed)
    ks = jax.random.split(key, 28)
    f32 = jnp.float32
    L = DEPTH

    def nrm(k, shape, s):
        return jax.random.normal(k, shape, f32) * s

    def gain(k, shape):
        return 1.0 + 0.05 * jax.random.normal(k, shape, f32)

    i_bias = nrm(ks[10], (L, 2, N_HEADS_M), 0.1)
    f_bias = jnp.linspace(F_BIAS_LO, F_BIAS_HI, N_HEADS_M, dtype=f32) + nrm(ks[11], (L, 2, N_HEADS_M), 0.1)
    return {
        'x': nrm(ks[0], (BATCH, SEQ, D_MODEL), 1.0),
        'c': nrm(ks[1], (BATCH, D_MODEL), 1.0),
        'ctx': nrm(ks[2], (BATCH, CTX_LEN, D_MODEL), 1.0),
        'c_ctx': nrm(ks[3], (D_MODEL,), 1.0),
        'w_ada': nrm(ks[4], (L, D_MODEL, N_MOD * D_MODEL), 0.5 * D_MODEL ** -0.5),
        'b_ada': nrm(ks[5], (L, N_MOD * D_MODEL), 0.01),
        'g_ffn1': gain(ks[6], (L, D_MODEL)),
        'w_ffn1_in': nrm(ks[7], (L, D_MODEL, 2 * D_FF), D_MODEL ** -0.5),
        'w_ffn1_out': nrm(ks[8], (L, D_FF, D_MODEL), D_FF ** -0.5),
        'g_mix': gain(ks[9], (L, D_MODEL)),
        'w_in': nrm(ks[12], (L, D_MODEL, D_PROJ), D_MODEL ** -0.5),
        'b_gates': jnp.stack([i_bias, f_bias], axis=2).reshape(L, 4 * N_HEADS_M),
        'conv_qk_w': nrm(ks[13], (L, CONV_W, 2 * D_MLSTM), CONV_W ** -0.5),
        'conv_qk_b': nrm(ks[14], (L, 2 * D_MLSTM), 0.01),
        'g_head': gain(ks[15], (L, D_MLSTM)),
        'g_sgu': gain(ks[16], (L, D_SGU)),
        'w_s': nrm(ks[17], (L, N_GROUPS_S, SGU_CHUNK, SGU_CHUNK), SGU_CHUNK ** -0.5),
        'b_s': 1.0 + nrm(ks[18], (L, N_GROUPS_S, SGU_CHUNK), 0.1),
        'w_branch_a': nrm(ks[19], (L, D_MLSTM, D_MODEL), D_MLSTM ** -0.5),
        'w_branch_b': nrm(ks[20], (L, D_SGU, D_MODEL), D_SGU ** -0.5),
        'w_out': nrm(ks[21], (L, D_MODEL, D_MODEL), D_MODEL ** -0.5),
        'g_ffn2': gain(ks[22], (L, D_MODEL)),
        'w_ffn2_in': nrm(ks[23], (L, D_MODEL, 2 * D_FF), D_MODEL ** -0.5),
        'w_ffn2_out': nrm(ks[24], (L, D_FF, D_MODEL), D_FF ** -0.5),
        'g_final': gain(ks[25], (D_MODEL,)),
    }


def reference(x, c, ctx, c_ctx, w_ada, b_ada, g_ffn1, w_ffn1_in, w_ffn1_out, g_mix, w_in, b_gates,
              conv_qk_w, conv_qk_b, g_head, g_sgu, w_s, b_s, w_branch_a, w_branch_b, w_out,
              g_ffn2, w_ffn2_in, w_ffn2_out, g_final):
    rows = x.shape[1] // GRID_W
    bsz = x.shape[0]
    h = x + grid_pos_emb(rows).astype(x.dtype)[None]
    hc = ctx
    for layer in range(DEPTH):
        is_last = layer == DEPTH - 1
        mods = jnp.split(jax.nn.silu(c) @ w_ada[layer] + b_ada[layer], N_MOD, axis=-1)
        mods_c = jnp.split(jax.nn.silu(c_ctx)[None] @ w_ada[layer] + b_ada[layer], N_MOD, axis=-1)

        h = ffn_sublayer(h, mods[0], mods[1], mods[2], g_ffn1[layer], w_ffn1_in[layer], w_ffn1_out[layer])
        hc = ffn_sublayer(hc, mods_c[0], mods_c[1], mods_c[2], g_ffn1[layer], w_ffn1_in[layer], w_ffn1_out[layer])

        p_lat = project(h, mods[3], mods[4], g_mix[layer], w_in[layer], PROJ_SIZES)
        ctx_sizes = PROJ_SIZES[:N_STATE_PIECES] if is_last else PROJ_SIZES
        p_ctx = project(hc, mods_c[3], mods_c[4], g_mix[layer], w_in[layer], ctx_sizes)
        q_l, k_l, v_l, gf_l, gb_l = mlstm_prep(*p_lat[:N_STATE_PIECES], conv_qk_w[layer], conv_qk_b[layer], b_gates[layer])
        q_c, k_c, v_c, gf_c, gb_c = mlstm_prep(*p_ctx[:N_STATE_PIECES], conv_qk_w[layer], conv_qk_b[layer], b_gates[layer])

        state0 = zero_state(bsz)
        hcf, st_f = mlstm_scan(q_c, k_c, v_c, gf_c[0], gf_c[1], state0, False, not is_last)
        hcb, st_b = mlstm_scan(q_c, k_c, v_c, gb_c[0], gb_c[1], state0, True, not is_last)
        hlf, _ = mlstm_scan(q_l, k_l, v_l, gf_l[0], gf_l[1], st_f, False, True)
        hlb, _ = mlstm_scan(q_l, k_l, v_l, gb_l[0], gb_l[1], st_b, True, True)

        y_lat = merge_branches(p_lat[N_STATE_PIECES:], hlf + hlb, g_head[layer], g_sgu[layer], w_s[layer], b_s[layer],
                               w_branch_a[layer], w_branch_b[layer], w_out[layer])
        h = h + mods[5][:, None, :] * y_lat
        if not is_last:
            y_ctx = merge_branches(p_ctx[N_STATE_PIECES:], hcf + hcb, g_head[layer], g_sgu[layer], w_s[layer], b_s[layer],
                                   w_branch_a[layer], w_branch_b[layer], w_out[layer])
            hc = hc + mods_c[5][:, None, :] * y_ctx
            hc = ffn_sublayer(hc, mods_c[6], mods_c[7], mods_c[8], g_ffn2[layer], w_ffn2_in[layer], w_ffn2_out[layer])

        h = ffn_sublayer(h, mods[6], mods[7], mods[8], g_ffn2[layer], w_ffn2_in[layer], w_ffn2_out[layer])
    return rmsnorm(h, g_final)
```

```python
import functools
import math

import jax
import jax.numpy as jnp
from jax import lax
from jax.experimental import pallas as pl
from jax.experimental.pallas import tpu as pltpu

F32 = jnp.float32
BF16 = jnp.bfloat16

D_MODEL = 1024
GRID_W = 64
N_HEADS = 4
HEAD_DIM = 256
N_GROUPS = 4
GROUP_DIM = 256
SGU_CHUNK = 128
D_FF = 2816
N_MOD = 9
N_GATES = 16
POS_BASE = 10000.0
EPS = 1e-6

LANES = 128
HALO = 8
VMEM_LIMIT = 56 * 1024 * 1024
MASK_NEG = -1e30

FFN_TM = 512
PROJ_TM = 512
MERGE_TM = 256
MLSTM_L = 128
FF_CHUNKS = ((0, 1024), (1024, 2048), (2048, D_FF))


def _dot(a, b):
    return jnp.dot(a, b, preferred_element_type=F32)


def _sigmoid(x):
    return 1.0 / (1.0 + jnp.exp(-x))


def _gelu_tanh(x):
    c = math.sqrt(2.0 / math.pi)
    return 0.5 * x * (1.0 + jnp.tanh(c * (x + 0.044715 * (x * x * x))))


def _log_sigmoid(x):
    return jnp.minimum(x, 0.0) - jnp.log(1.0 + jnp.exp(-jnp.abs(x)))


def _rms(x, g):
    return x * lax.rsqrt(jnp.mean(x * x, axis=-1, keepdims=True) + EPS) * g


def _norm_mod(x, g, shift, scale):
    return _rms(x, g) * (1.0 + scale) + shift


def _const_spec(shape):
    zeros = (0,) * len(shape)
    return pl.BlockSpec(shape, lambda *_: zeros, pipeline_mode=pl.Buffered(1))


def _params(sem):
    return pltpu.CompilerParams(dimension_semantics=sem, vmem_limit_bytes=VMEM_LIMIT)


def _mods_kernel(c_ref, w_ref, b_ref, o_ref):
    c = c_ref[...]
    s = (c * _sigmoid(c)).astype(BF16)
    o_ref[...] = _dot(s, w_ref[...].astype(BF16)) + b_ref[...]


def _mods_call(cond, w_ada, b_ada):
    rows, d = cond.shape
    n = w_ada.shape[1]
    tn = 1024
    return pl.pallas_call(
        _mods_kernel,
        out_shape=jax.ShapeDtypeStruct((rows, n), F32),
        grid=(n // tn,),
        in_specs=[
            pl.BlockSpec((rows, d), lambda j: (0, 0)),
            pl.BlockSpec((d, tn), lambda j: (0, j)),
            pl.BlockSpec((1, tn), lambda j: (0, j)),
        ],
        out_specs=pl.BlockSpec((rows, tn), lambda j: (0, j)),
        compiler_params=_params(("arbitrary",)),
        name="mods",
    )(cond, w_ada, b_ada.reshape(1, n))


def _ffn_kernel(*refs, mod_base, add_pos, final_norm):
    it = iter(refs)
    x_ref = next(it)
    pos_ref = next(it) if add_pos else None
    mods_ref, g_ref, win_ref, wout_ref = next(it), next(it), next(it), next(it)
    gfin_ref = next(it) if final_norm else None
    o_ref = next(it)

    x = x_ref[0]
    if add_pos:
        x = x + pos_ref[...]
    m = mods_ref[0]
    shift = m[mod_base:mod_base + 1]
    scale = m[mod_base + 1:mod_base + 2]
    gate = m[mod_base + 2:mod_base + 3]
    hb = _norm_mod(x, g_ref[...], shift, scale).astype(BF16)
    y = None
    for c0, c1 in FF_CHUNKS:
        a = _dot(hb, win_ref[:, c0:c1])
        b = _dot(hb, win_ref[:, D_FF + c0:D_FF + c1])
        act = (a * _sigmoid(a) * b).astype(BF16)
        part = _dot(act, wout_ref[c0:c1, :])
        y = part if y is None else y + part
    out = x + 0.5 * gate * y
    if final_norm:
        out = _rms(out, gfin_ref[...])
    o_ref[0] = out


def _ffn_call(x, pos, mods, mod_row, g, w_in, w_out, g_final, *, mod_base, tm, name):
    bsz, s, d = x.shape
    add_pos = pos is not None
    final_norm = g_final is not None
    args = [x]
    specs = [pl.BlockSpec((1, tm, d), lambda i, b: (b, i, 0))]
    if add_pos:
        args.append(pos)
        specs.append(pl.BlockSpec((tm, d), lambda i, b: (i, 0)))
    args += [mods, g.reshape(1, d), w_in, w_out]
    specs += [
        pl.BlockSpec((1, N_MOD, d), lambda i, b: (mod_row(b), 0, 0)),
        _const_spec((1, d)),
        _const_spec(w_in.shape),
        _const_spec(w_out.shape),
    ]
    if final_norm:
        args.append(g_final.reshape(1, d))
        specs.append(_const_spec((1, d)))
    kern = functools.partial(_ffn_kernel, mod_base=mod_base, add_pos=add_pos, final_norm=final_norm)
    return pl.pallas_call(
        kern,
        out_shape=jax.ShapeDtypeStruct((bsz, s, d), F32),
        grid=(s // tm, bsz),
        in_specs=specs,
        out_specs=pl.BlockSpec((1, tm, d), lambda i, b: (b, i, 0)),
        compiler_params=_params(("arbitrary", "arbitrary")),
        name=name,
    )(*args)


def _proj_kernel(*refs, tm, with_rest):
    it = iter(refs)
    h_ref, hp_ref, hx_ref, mods_ref, g_ref = (next(it) for _ in range(5))
    wqk_ref, wv_ref, wg_ref, wgt_ref = (next(it) for _ in range(4))
    wrest_ref = next(it) if with_rest else None
    cw_ref, cb_ref, bgc_ref, bgr_ref = (next(it) for _ in range(4))
    q_ref, k_ref, kt_ref, v_ref, gc_ref, gr_ref = (next(it) for _ in range(6))
    rest_refs = [next(it) for _ in range(5)] if with_rest else []
    raw_ref = next(it)

    i = pl.program_id(0)
    nt = pl.num_programs(0)
    m = mods_ref[0]
    shift, scale = m[3:4], m[4:5]
    g = g_ref[...]
    hn = _norm_mod(h_ref[0], g, shift, scale)
    prev_ok = (i > 0).astype(F32)
    next_ok = (i < nt - 1).astype(F32)
    hn_prev = _norm_mod(hp_ref[0], g, shift, scale) * prev_ok
    hn_next = _norm_mod(hx_ref[0], g, shift, scale) * next_ok
    hb = hn.astype(BF16)
    ext = jnp.concatenate([hn_prev, hn, hn_next], axis=0).astype(BF16)

    raw_ref[...] = _dot(ext, wqk_ref[...])
    for c in range(2 * N_HEADS):
        cols = slice(c * HEAD_DIM, (c + 1) * HEAD_DIM)
        cw = cw_ref[:, cols]
        acc = raw_ref[pl.ds(HALO - 1, tm), cols] * cw[0:1]
        acc = acc + raw_ref[pl.ds(HALO, tm), cols] * cw[1:2]
        acc = acc + raw_ref[pl.ds(HALO + 1, tm), cols] * cw[2:3]
        acc = acc + cb_ref[:, cols]
        act = acc * _sigmoid(acc)
        if c < N_HEADS:
            q_ref[0, :, cols] = (act * HEAD_DIM ** -0.5).astype(BF16)
        else:
            kc = slice((c - N_HEADS) * HEAD_DIM, (c - N_HEADS + 1) * HEAD_DIM)
            k_ref[0, :, kc] = act.astype(BF16)
            kt_ref[0, kc, :] = act.T.astype(BF16)

    v_ref[0] = _dot(hb, wv_ref[...]).astype(BF16)

    gcol = _dot(hb, wg_ref[...]) + bgc_ref[...]
    lane = lax.broadcasted_iota(jnp.int32, gcol.shape, 1)
    gc_ref[0] = jnp.where(lane % 8 >= N_HEADS, _log_sigmoid(gcol), gcol)
    grow = lax.dot_general(wgt_ref[...], hb, (((1,), (1,)), ((), ())),
                           preferred_element_type=F32) + bgr_ref[...]
    sub = lax.broadcasted_iota(jnp.int32, grow.shape, 0)
    gr_ref[0] = jnp.where(sub % 8 >= N_HEADS, _log_sigmoid(grow), grow)

    if with_rest:
        acts = (_sigmoid, _gelu_tanh, _gelu_tanh, _sigmoid, _sigmoid)
        for j, (ref, fn) in enumerate(zip(rest_refs, acts)):
            p = _dot(hb, wrest_ref[:, j * D_MODEL:(j + 1) * D_MODEL])
            ref[0] = fn(p).astype(BF16)


def _proj_call(h, mods, mod_row, g, wqk, wv, wg, wgt, wrest, conv_w, conv_b, bg_col, bg_row, *, tm, name):
    bsz, s, d = h.shape
    with_rest = wrest is not None
    nhb = tm // HALO
    last = s // HALO - 1
    args = [h, h, h, mods, g.reshape(1, d), wqk, wv, wg, wgt]
    specs = [
        pl.BlockSpec((1, tm, d), lambda i, b: (b, i, 0)),
        pl.BlockSpec((1, HALO, d), lambda i, b: (b, jnp.maximum(i * nhb - 1, 0), 0)),
        pl.BlockSpec((1, HALO, d), lambda i, b: (b, jnp.minimum((i + 1) * nhb, last), 0)),
        pl.BlockSpec((1, N_MOD, d), lambda i, b: (mod_row(b), 0, 0)),
        _const_spec((1, d)),
        _const_spec(wqk.shape), _const_spec(wv.shape), _const_spec(wg.shape), _const_spec(wgt.shape),
    ]
    if with_rest:
        args.append(wrest)
        specs.append(_const_spec(wrest.shape))
    args += [conv_w, conv_b.reshape(1, -1), bg_col, bg_row]
    specs += [_const_spec(conv_w.shape), _const_spec((1, conv_b.shape[0])),
              _const_spec(bg_col.shape), _const_spec(bg_row.shape)]
    tok = lambda i, b: (b, i, 0)
    out_shape = [
        jax.ShapeDtypeStruct((bsz, s, d), BF16),
        jax.ShapeDtypeStruct((bsz, s, d), BF16),
        jax.ShapeDtypeStruct((bsz, d, s), BF16),
        jax.ShapeDtypeStruct((bsz, s, d), BF16),
        jax.ShapeDtypeStruct((bsz, s, LANES), F32),
        jax.ShapeDtypeStruct((bsz, N_GATES, s), F32),
    ]
    out_specs = [
        pl.BlockSpec((1, tm, d), tok), pl.BlockSpec((1, tm, d), tok),
        pl.BlockSpec((1, d, tm), lambda i, b: (b, 0, i)),
        pl.BlockSpec((1, tm, d), tok),
        pl.BlockSpec((1, tm, LANES), tok),
        pl.BlockSpec((1, N_GATES, tm), lambda i, b: (b, 0, i)),
    ]
    if with_rest:
        out_shape += [jax.ShapeDtypeStruct((bsz, s, d), BF16)] * 5
        out_specs += [pl.BlockSpec((1, tm, d), tok)] * 5
    kern = functools.partial(_proj_kernel, tm=tm, with_rest=with_rest)
    return pl.pallas_call(
        kern,
        out_shape=out_shape,
        grid=(s // tm, bsz),
        in_specs=specs,
        out_specs=out_specs,
        scratch_shapes=[pltpu.VMEM((tm + 2 * HALO, 2 * d), F32)],
        compiler_params=_params(("arbitrary", "arbitrary")),
        name=name,
    )(*args)


def _split3(x):
    hi = x.astype(BF16)
    r1 = x - hi.astype(F32)
    mid = r1.astype(BF16)
    lo = (r1 - mid.astype(F32)).astype(BF16)
    return hi, mid, lo


def _cumsum_cols(tri, x):
    hi, mid, lo = _split3(x)
    return _dot(tri, hi) + _dot(tri, mid) + _dot(tri, lo)


def _cumsum_rows(x, tri):
    hi, mid, lo = _split3(x)
    return _dot(hi, tri) + _dot(mid, tri) + _dot(lo, tri)


def _mlstm_kernel(*refs, chunk, nb, emit, has_init):
    it = iter(refs)
    dirs = []
    for _ in range(2):
        dirs.append(tuple(next(it) for _ in range(6)))
    init = tuple(next(it) for _ in range(3)) if has_init else None
    h_refs = (next(it), next(it)) if emit else None
    c_ref, n_ref, m_ref = next(it), next(it), next(it)

    step = pl.program_id(0)

    @pl.when(step == 0)
    def _():
        if has_init:
            c_ref[...] = init[0][...]
            n_ref[...] = init[1][...]
            m_ref[...] = init[2][...]
        else:
            c_ref[...] = jnp.zeros_like(c_ref)
            n_ref[...] = jnp.zeros_like(n_ref)
            m_ref[...] = jnp.zeros_like(m_ref)

    row = lax.broadcasted_iota(jnp.int32, (chunk, chunk), 0)
    col = lax.broadcasted_iota(jnp.int32, (chunk, chunk), 1)
    lower = row >= col
    upper = row <= col
    tri_l = jnp.where(lower, 1.0, 0.0).astype(BF16)
    tri_u = jnp.where(upper, 1.0, 0.0).astype(BF16)

    for d, (q_ref, kt_ref, k_ref, v_ref, gc_ref, gr_ref) in enumerate(dirs):
        mask = lower if d == 0 else upper
        tri_col = tri_l if d == 0 else tri_u
        tri_row = tri_u if d == 0 else tri_l
        last = chunk - 1 if d == 0 else 0
        for b in range(nb):
            gc = gc_ref[b]
            gr = gr_ref[b]
            b_col_all = _cumsum_cols(tri_col, gc)
            b_row_all = _cumsum_rows(gr, tri_row)
            for hd in range(N_HEADS):
                j = (d * nb + b) * N_HEADS + hd
                gi = 8 * d + hd
                gf = gi + N_HEADS
                lanes = slice(hd * HEAD_DIM, (hd + 1) * HEAD_DIM)
                b_col = b_col_all[:, gf:gf + 1]
                b_row = b_row_all[gf:gf + 1, :]
                r_row = gr[gi:gi + 1, :] - b_row
                g_tot = b_row[:, last:last + 1]
                m_st = m_ref[j]
                n_st = n_ref[j]
                q = q_ref[b, :, lanes]
                kt = kt_ref[b, lanes, :]
                k = k_ref[b, :, lanes]
                v = v_ref[b, :, lanes]
                ct = c_ref[j]

                if emit:
                    dm = jnp.where(mask, b_col + r_row, MASK_NEG)
                    a = b_col + m_st
                    m_t = jnp.maximum(a, jnp.max(dm, axis=-1, keepdims=True))
                    p = jnp.exp(dm - m_t)
                    inter = jnp.exp(a - m_t)
                    s_ts = _dot(q, kt) * p
                    num = _dot(s_ts.astype(BF16), v) + inter * _dot(q, ct.astype(BF16))
                    qn = jnp.sum(q.astype(F32) * n_st, axis=-1, keepdims=True)
                    den = jnp.sum(s_ts, axis=-1, keepdims=True) + inter * qn
                    h = num / jnp.maximum(jnp.abs(den), jnp.exp(-m_t))
                    h_refs[d][b, :, lanes] = h.astype(BF16)

                m_new = jnp.maximum(g_tot + m_st, g_tot + jnp.max(r_row, axis=-1, keepdims=True))
                ws = jnp.exp(g_tot + r_row - m_new)
                decay = jnp.exp(g_tot + m_st - m_new)
                ktw = (kt.astype(F32) * ws).astype(BF16)
                c_ref[j] = decay * ct + _dot(ktw, v)
                n_ref[j] = decay * n_st + _dot(ws.astype(BF16), k)
                m_ref[j] = m_new


def _mlstm_call(q, kt, k, v, gc, gr, init, *, chunk, emit, name):
    bsz, s, d = q.shape
    nc = s // chunk
    nchain = 2 * bsz * N_HEADS
    has_init = init is not None

    def dir_specs(idx):
        return [
            pl.BlockSpec((bsz, chunk, d), lambda i: (0, idx(i), 0)),
            pl.BlockSpec((bsz, d, chunk), lambda i: (0, 0, idx(i))),
            pl.BlockSpec((bsz, chunk, d), lambda i: (0, idx(i), 0)),
            pl.BlockSpec((bsz, chunk, d), lambda i: (0, idx(i), 0)),
            pl.BlockSpec((bsz, chunk, LANES), lambda i: (0, idx(i), 0)),
            pl.BlockSpec((bsz, N_GATES, chunk), lambda i: (0, 0, idx(i))),
        ]

    fwd = lambda i: i
    bwd = lambda i: nc - 1 - i
    args = [q, kt, k, v, gc, gr] * 2
    specs = dir_specs(fwd) + dir_specs(bwd)
    state_shapes = [(nchain, HEAD_DIM, HEAD_DIM), (nchain, 1, HEAD_DIM), (nchain, 1, 1)]
    if has_init:
        args += list(init)
        specs += [_const_spec(sh) for sh in state_shapes]
    kern = functools.partial(_mlstm_kernel, chunk=chunk, nb=bsz, emit=emit, has_init=has_init)
    if emit:
        out_shape = [jax.ShapeDtypeStruct((bsz, s, d), BF16)] * 2
        out_specs = [pl.BlockSpec((bsz, chunk, d), lambda i: (0, fwd(i), 0)),
                     pl.BlockSpec((bsz, chunk, d), lambda i: (0, bwd(i), 0))]
        scratch = [pltpu.VMEM(sh, F32) for sh in state_shapes]
    else:
        out_shape = [jax.ShapeDtypeStruct(sh, F32) for sh in state_shapes]
        out_specs = [pl.BlockSpec(sh, lambda i, n=len(sh): (0,) * n) for sh in state_shapes]
        scratch = []
    return pl.pallas_call(
        kern,
        out_shape=out_shape,
        grid=(nc,),
        in_specs=specs,
        out_specs=out_specs,
        scratch_shapes=scratch,
        compiler_params=_params(("arbitrary",)),
        name=name,
    )(*args)


def _merge_kernel(hf_ref, hb_ref, so_ref, gu_ref, gv_ref, sa_ref, sb_ref, h_ref, mods_ref,
                  ghead_ref, gsgu_ref, ws_ref, bs_ref, wa_ref, wb_ref, wo_ref, o_ref, yb_ref, *, tm):
    hm = hf_ref[0].astype(F32) + hb_ref[0].astype(F32)
    parts = []
    for hd in range(N_HEADS):
        x = hm[:, hd * HEAD_DIM:(hd + 1) * HEAD_DIM]
        xc = x - jnp.mean(x, axis=-1, keepdims=True)
        parts.append(xc * lax.rsqrt(jnp.mean(xc * xc, axis=-1, keepdims=True) + EPS))
    ln = jnp.concatenate(parts, axis=-1) * ghead_ref[...]
    y_a = (so_ref[0].astype(F32) * ln).astype(BF16)

    vn = _rms(gv_ref[0].astype(F32), gsgu_ref[...]).astype(BF16)
    for r in range(tm // SGU_CHUNK):
        rows = slice(r * SGU_CHUNK, (r + 1) * SGU_CHUNK)
        for gidx in range(N_GROUPS):
            cols = slice(gidx * GROUP_DIM, (gidx + 1) * GROUP_DIM)
            mixed = _dot(ws_ref[gidx], vn[rows, cols]) + bs_ref[:, gidx:gidx + 1]
            yb_ref[rows, cols] = (gu_ref[0, rows, cols].astype(F32) * mixed).astype(BF16)

    mixed_out = (sa_ref[0].astype(F32) * _dot(y_a, wa_ref[...])
                 + sb_ref[0].astype(F32) * _dot(yb_ref[...], wb_ref[...]))
    y = _dot(mixed_out.astype(BF16), wo_ref[...])
    o_ref[0] = h_ref[0] + mods_ref[0][5:6] * y


def _merge_call(hf, hb, so, gu, gv, sa, sb, h, mods, g_head, g_sgu, w_s, b_s_t, w_a, w_b, w_o, *, tm):
    bsz, s, d = h.shape
    tok = pl.BlockSpec((1, tm, d), lambda i, b: (b, i, 0))
    specs = [tok] * 8 + [
        pl.BlockSpec((1, N_MOD, d), lambda i, b: (b, 0, 0)),
        _const_spec((1, d)), _const_spec((1, d)),
        _const_spec(w_s.shape), _const_spec(b_s_t.shape),
        _const_spec(w_a.shape), _const_spec(w_b.shape), _const_spec(w_o.shape),
    ]
    return pl.pallas_call(
        functools.partial(_merge_kernel, tm=tm),
        out_shape=jax.ShapeDtypeStruct((bsz, s, d), F32),
        grid=(s // tm, bsz),
        in_specs=specs,
        out_specs=tok,
        scratch_shapes=[pltpu.VMEM((tm, d), BF16)],
        compiler_params=_params(("arbitrary", "arbitrary")),
        name="merge",
    )(hf, hb, so, gu, gv, sa, sb, h, mods, g_head.reshape(1, d), g_sgu.reshape(1, d),
      w_s, b_s_t, w_a, w_b, w_o)


def _grid_pos_emb(rows):
    t = jnp.arange(rows * GRID_W)
    r = (t // GRID_W).astype(F32)
    col = (t % GRID_W).astype(F32)
    quarter = D_MODEL // 4
    freqs = jnp.exp(-math.log(POS_BASE) * jnp.arange(quarter, dtype=F32) / quarter)
    ar = r[:, None] * freqs
    ac = col[:, None] * freqs
    return jnp.concatenate([jnp.sin(ar), jnp.cos(ar), jnp.sin(ac), jnp.cos(ac)], axis=-1)


def kernel(x, c, ctx, c_ctx, w_ada, b_ada, g_ffn1, w_ffn1_in, w_ffn1_out, g_mix, w_in, b_gates,
           conv_qk_w, conv_qk_b, g_head, g_sgu, w_s, b_s, w_branch_a, w_branch_b, w_out,
           g_ffn2, w_ffn2_in, w_ffn2_out, g_final):
    bsz, seq, d = x.shape
    ctx_len = ctx.shape[1]
    layer = 0
    pos = _grid_pos_emb(seq // GRID_W)

    cond = jnp.concatenate([c, c_ctx[None], jnp.zeros((HALO - bsz - 1, d), F32)], axis=0)
    mods = _mods_call(cond, w_ada[layer], b_ada[layer]).reshape(HALO, N_MOD, d)
    lat_row = lambda b: b
    ctx_row = lambda b: bsz

    w1_in = w_ffn1_in[layer].astype(BF16)
    w1_out = w_ffn1_out[layer].astype(BF16)
    w2_in = w_ffn2_in[layer].astype(BF16)
    w2_out = w_ffn2_out[layer].astype(BF16)
    wi = w_in[layer]
    wqk = wi[:, :2 * d].astype(BF16)
    wv = wi[:, 2 * d:3 * d].astype(BF16)
    wgates = wi[:, 3 * d:3 * d + N_GATES]
    wg = jnp.pad(wgates, ((0, 0), (0, LANES - N_GATES))).astype(BF16)
    wgt = wgates.T.astype(BF16)
    wrest = wi[:, 3 * d + N_GATES:].astype(BF16)
    bg = b_gates[layer]
    bg_col = jnp.pad(bg, (0, LANES - N_GATES)).reshape(1, LANES)

    def bg_row(tm):
        return jnp.broadcast_to(bg[:, None], (N_GATES, tm))

    h = _ffn_call(x, pos, mods, lat_row, g_ffn1[layer], w1_in, w1_out, None,
                  mod_base=0, tm=FFN_TM, name="ffn1")
    hc = _ffn_call(ctx, None, mods, ctx_row, g_ffn1[layer], w1_in, w1_out, None,
                   mod_base=0, tm=ctx_len, name="ffn1_ctx")

    q_c, k_c, kt_c, v_c, gc_c, gr_c = _proj_call(
        hc, mods, ctx_row, g_mix[layer], wqk, wv, wg, wgt, None, conv_qk_w[layer], conv_qk_b[layer],
        bg_col, bg_row(ctx_len), tm=ctx_len, name="proj_ctx")
    q_l, k_l, kt_l, v_l, gc_l, gr_l, so, gu, gv, sa, sb = _proj_call(
        h, mods, lat_row, g_mix[layer], wqk, wv, wg, wgt, wrest, conv_qk_w[layer], conv_qk_b[layer],
        bg_col, bg_row(PROJ_TM), tm=PROJ_TM, name="proj")

    state = _mlstm_call(q_c, kt_c, k_c, v_c, gc_c, gr_c, None, chunk=MLSTM_L, emit=False, name="mlstm_ctx")
    hf, hb = _mlstm_call(q_l, kt_l, k_l, v_l, gc_l, gr_l, state, chunk=MLSTM_L, emit=True, name="mlstm")

    h = _merge_call(hf, hb, so, gu, gv, sa, sb, h, mods, g_head[layer], g_sgu[layer],
                    w_s[layer].astype(BF16), b_s[layer].T, w_branch_a[layer].astype(BF16),
                    w_branch_b[layer].astype(BF16), w_out[layer].astype(BF16), tm=MERGE_TM)

    return _ffn_call(h, None, mods, lat_row, g_ffn2[layer], w2_in, w2_out, g_final,
                     mod_base=6, tm=FFN_TM, name="ffn2")
```

```python
import functools
import math

import jax
import jax.numpy as jnp
from jax import lax
from jax.experimental import pallas as pl
from jax.experimental.pallas import tpu as pltpu

F32 = jnp.float32
BF16 = jnp.bfloat16

D_MODEL = 1024
GRID_W = 64
N_HEADS = 4
HEAD_DIM = 256
N_GROUPS = 4
GROUP_DIM = 256
SGU_CHUNK = 128
D_FF = 2816
N_MOD = 9
N_GATES = 16
POS_BASE = 10000.0
EPS = 1e-6

LANES = 128
HALO = 8
VMEM_LIMIT = 56 * 1024 * 1024
MASK_NEG = -1e30

FFN_TM = 512
PROJ_TM = 512
MERGE_TM = 256
MLSTM_L = 128
FF_CHUNKS = ((0, 1024), (1024, 2048), (2048, D_FF))


def _dot(a, b):
    return jnp.dot(a, b, preferred_element_type=F32)


def _sigmoid(x):
    return 1.0 / (1.0 + jnp.exp(-x))


def _gelu_tanh(x):
    c = math.sqrt(2.0 / math.pi)
    return 0.5 * x * (1.0 + jnp.tanh(c * (x + 0.044715 * (x * x * x))))


def _log_sigmoid(x):
    return jnp.minimum(x, 0.0) - jnp.log(1.0 + jnp.exp(-jnp.abs(x)))


def _rms(x, g):
    return x * lax.rsqrt(jnp.mean(x * x, axis=-1, keepdims=True) + EPS) * g


def _norm_mod(x, g, shift, scale):
    return _rms(x, g) * (1.0 + scale) + shift


def _const_spec(shape):
    zeros = (0,) * len(shape)
    return pl.BlockSpec(shape, lambda *_: zeros, pipeline_mode=pl.Buffered(1))


def _params(sem):
    return pltpu.CompilerParams(dimension_semantics=sem, vmem_limit_bytes=VMEM_LIMIT)


def _mods_kernel(c_ref, w_ref, b_ref, o_ref):
    c = c_ref[...]
    s = (c * _sigmoid(c)).astype(BF16)
    o_ref[...] = _dot(s, w_ref[...].astype(BF16)) + b_ref[...]


def _mods_call(cond, w_ada, b_ada):
    rows, d = cond.shape
    n = w_ada.shape[1]
    tn = 1024
    return pl.pallas_call(
        _mods_kernel,
        out_shape=jax.ShapeDtypeStruct((rows, n), F32),
        grid=(n // tn,),
        in_specs=[
            pl.BlockSpec((rows, d), lambda j: (0, 0)),
            pl.BlockSpec((d, tn), lambda j: (0, j)),
            pl.BlockSpec((1, tn), lambda j: (0, j)),
        ],
        out_specs=pl.BlockSpec((rows, tn), lambda j: (0, j)),
        compiler_params=_params(("arbitrary",)),
        name="mods",
    )(cond, w_ada, b_ada.reshape(1, n))


def _ffn_kernel(*refs, tm, mod_base, add_pos, final_norm):
    it = iter(refs)
    x_ref = next(it)
    prow_ref, pcol_ref = (next(it), next(it)) if add_pos else (None, None)
    mods_ref, g_ref, win_ref, wout_ref = next(it), next(it), next(it), next(it)
    gfin_ref = next(it) if final_norm else None
    o_ref = next(it)

    x = x_ref[0]
    if add_pos:
        pcol = pcol_ref[...]
        half = pcol.shape[1]
        pos = [jnp.concatenate([jnp.broadcast_to(prow_ref[r:r + 1, :], (GRID_W, half)), pcol], axis=-1)
               for r in range(tm // GRID_W)]
        x = x + jnp.concatenate(pos, axis=0)
    m = mods_ref[0]
    shift = m[mod_base:mod_base + 1]
    scale = m[mod_base + 1:mod_base + 2]
    gate = m[mod_base + 2:mod_base + 3]
    hb = _norm_mod(x, g_ref[...], shift, scale).astype(BF16)
    y = None
    for c0, c1 in FF_CHUNKS:
        a = _dot(hb, win_ref[:, c0:c1])
        b = _dot(hb, win_ref[:, D_FF + c0:D_FF + c1])
        act = (a * _sigmoid(a) * b).astype(BF16)
        part = _dot(act, wout_ref[c0:c1, :])
        y = part if y is None else y + part
    out = x + 0.5 * gate * y
    if final_norm:
        out = _rms(out, gfin_ref[...])
    o_ref[0] = out


def _ffn_call(x, pos, mods, mod_row, g, w_in, w_out, g_final, *, mod_base, tm, name):
    bsz, s, d = x.shape
    add_pos = pos is not None
    final_norm = g_final is not None
    args = [x]
    specs = [pl.BlockSpec((1, tm, d), lambda i, b: (b, i, 0))]
    if add_pos:
        prow, pcol = pos
        args += [prow, pcol]
        specs += [pl.BlockSpec((tm // GRID_W, prow.shape[1]), lambda i, b: (i, 0)),
                  _const_spec(pcol.shape)]
    args += [mods, g.reshape(1, d), w_in, w_out]
    specs += [
        pl.BlockSpec((1, N_MOD, d), lambda i, b: (mod_row(b), 0, 0)),
        _const_spec((1, d)),
        _const_spec(w_in.shape),
        _const_spec(w_out.shape),
    ]
    if final_norm:
        args.append(g_final.reshape(1, d))
        specs.append(_const_spec((1, d)))
    kern = functools.partial(_ffn_kernel, tm=tm, mod_base=mod_base, add_pos=add_pos, final_norm=final_norm)
    return pl.pallas_call(
        kern,
        out_shape=jax.ShapeDtypeStruct((bsz, s, d), F32),
        grid=(s // tm, bsz),
        in_specs=specs,
        out_specs=pl.BlockSpec((1, tm, d), lambda i, b: (b, i, 0)),
        compiler_params=_params(("arbitrary", "arbitrary")),
        name=name,
    )(*args)


def _proj_kernel(*refs, tm, with_rest):
    it = iter(refs)
    h_ref, hp_ref, hx_ref, mods_ref, g_ref = (next(it) for _ in range(5))
    wqk_ref, wv_ref, wgt_ref = (next(it) for _ in range(3))
    wrest_ref = next(it) if with_rest else None
    cw_ref, cb_ref, bgr_ref = (next(it) for _ in range(3))
    q_ref, k_ref, kt_ref, v_ref, gr_ref = (next(it) for _ in range(5))
    rest_refs = [next(it) for _ in range(5)] if with_rest else []
    raw_ref = next(it)

    i = pl.program_id(0)
    nt = pl.num_programs(0)
    m = mods_ref[0]
    shift, scale = m[3:4], m[4:5]
    g = g_ref[...]
    hn = _norm_mod(h_ref[0], g, shift, scale)
    prev_ok = (i > 0).astype(F32)
    next_ok = (i < nt - 1).astype(F32)
    hn_prev = _norm_mod(hp_ref[0], g, shift, scale) * prev_ok
    hn_next = _norm_mod(hx_ref[0], g, shift, scale) * next_ok
    hb = hn.astype(BF16)
    ext = jnp.concatenate([hn_prev, hn, hn_next], axis=0).astype(BF16)

    raw_ref[...] = _dot(ext, wqk_ref[...])
    for c in range(2 * N_HEADS):
        cols = slice(c * HEAD_DIM, (c + 1) * HEAD_DIM)
        cw = cw_ref[:, cols]
        acc = raw_ref[pl.ds(HALO - 1, tm), cols] * cw[0:1]
        acc = acc + raw_ref[pl.ds(HALO, tm), cols] * cw[1:2]
        acc = acc + raw_ref[pl.ds(HALO + 1, tm), cols] * cw[2:3]
        acc = acc + cb_ref[:, cols]
        act = acc * _sigmoid(acc)
        if c < N_HEADS:
            q_ref[0, :, cols] = (act * HEAD_DIM ** -0.5).astype(BF16)
        else:
            kc = slice((c - N_HEADS) * HEAD_DIM, (c - N_HEADS + 1) * HEAD_DIM)
            k_ref[0, :, kc] = act.astype(BF16)
            kt_ref[0, kc, :] = act.T.astype(BF16)

    v_ref[0] = _dot(hb, wv_ref[...]).astype(BF16)

    grow = lax.dot_general(wgt_ref[...], hb, (((1,), (1,)), ((), ())),
                           preferred_element_type=F32) + bgr_ref[...]
    sub = lax.broadcasted_iota(jnp.int32, grow.shape, 0)
    gr_ref[0] = jnp.where(sub % 8 >= N_HEADS, _log_sigmoid(grow), grow)

    if with_rest:
        acts = (_sigmoid, _gelu_tanh, _gelu_tanh, _sigmoid, _sigmoid)
        for j, (ref, fn) in enumerate(zip(rest_refs, acts)):
            p = _dot(hb, wrest_ref[:, j * D_MODEL:(j + 1) * D_MODEL])
            ref[0] = fn(p).astype(BF16)


def _proj_call(h, mods, mod_row, g, wqk, wv, wgt, wrest, conv_w, conv_b, bg_row, *, tm, name):
    bsz, s, d = h.shape
    with_rest = wrest is not None
    nhb = tm // HALO
    last = s // HALO - 1
    args = [h, h, h, mods, g.reshape(1, d), wqk, wv, wgt]
    specs = [
        pl.BlockSpec((1, tm, d), lambda i, b: (b, i, 0)),
        pl.BlockSpec((1, HALO, d), lambda i, b: (b, jnp.maximum(i * nhb - 1, 0), 0)),
        pl.BlockSpec((1, HALO, d), lambda i, b: (b, jnp.minimum((i + 1) * nhb, last), 0)),
        pl.BlockSpec((1, N_MOD, d), lambda i, b: (mod_row(b), 0, 0)),
        _const_spec((1, d)),
        _const_spec(wqk.shape), _const_spec(wv.shape), _const_spec(wgt.shape),
    ]
    if with_rest:
        args.append(wrest)
        specs.append(_const_spec(wrest.shape))
    args += [conv_w, conv_b.reshape(1, -1), bg_row]
    specs += [_const_spec(conv_w.shape), _const_spec((1, conv_b.shape[0])), _const_spec(bg_row.shape)]
    tok = lambda i, b: (b, i, 0)
    out_shape = [
        jax.ShapeDtypeStruct((bsz, s, d), BF16),
        jax.ShapeDtypeStruct((bsz, s, d), BF16),
        jax.ShapeDtypeStruct((bsz, d, s), BF16),
        jax.ShapeDtypeStruct((bsz, s, d), BF16),
        jax.ShapeDtypeStruct((bsz, N_GATES, s), F32),
    ]
    out_specs = [
        pl.BlockSpec((1, tm, d), tok), pl.BlockSpec((1, tm, d), tok),
        pl.BlockSpec((1, d, tm), lambda i, b: (b, 0, i)),
        pl.BlockSpec((1, tm, d), tok),
        pl.BlockSpec((1, N_GATES, tm), lambda i, b: (b, 0, i)),
    ]
    if with_rest:
        out_shape += [jax.ShapeDtypeStruct((bsz, s, d), BF16)] * 5
        out_specs += [pl.BlockSpec((1, tm, d), tok)] * 5
    kern = functools.partial(_proj_kernel, tm=tm, with_rest=with_rest)
    return pl.pallas_call(
        kern,
        out_shape=out_shape,
        grid=(s // tm, bsz),
        in_specs=specs,
        out_specs=out_specs,
        scratch_shapes=[pltpu.VMEM((tm + 2 * HALO, 2 * d), F32)],
        compiler_params=_params(("arbitrary", "arbitrary")),
        name=name,
    )(*args)


def _split3(x):
    hi = x.astype(BF16)
    r1 = x - hi.astype(F32)
    mid = r1.astype(BF16)
    lo = (r1 - mid.astype(F32)).astype(BF16)
    return hi, mid, lo


def _chain(d, b, hd, nb):
    return (d * nb + b) * N_HEADS + hd


def _mgate_kernel(*refs, chunk, nb, emit, has_init):
    it = iter(refs)
    gr_refs = (next(it), next(it))
    m0_ref = next(it) if has_init else None
    p_refs = (next(it), next(it)) if emit else None
    col_refs = (next(it), next(it)) if emit else None
    ws_refs = (next(it), next(it))
    dec_ref = next(it)
    m_ref = next(it)

    @pl.when(pl.program_id(0) == 0)
    def _():
        m_ref[...] = m0_ref[...] if has_init else jnp.zeros_like(m_ref)

    row = lax.broadcasted_iota(jnp.int32, (chunk, chunk), 0)
    col = lax.broadcasted_iota(jnp.int32, (chunk, chunk), 1)
    lane = lax.broadcasted_iota(jnp.int32, (N_HEADS, chunk), 1)
    ones = jnp.ones((chunk, chunk), BF16)

    for d, gr_ref in enumerate(gr_refs):
        visible = (row >= col) if d == 0 else (row <= col)
        tri = jnp.where((row <= col) if d == 0 else (row >= col), 1.0, 0.0).astype(BF16)
        for b in range(nb):
            grp = d * nb + b
            gr = gr_ref[b]
            pieces = _split3(gr)
            b_all = sum(_dot(x, tri) for x in pieces)
            g_all = sum(_dot(x, ones) for x in pieces)
            li = gr[8 * d:8 * d + N_HEADS]
            bq = b_all[8 * d + N_HEADS:8 * d + 2 * N_HEADS]
            g4 = g_all[8 * d + N_HEADS:8 * d + 2 * N_HEADS]
            r4 = li - bq
            m_st = m_ref[grp]
            m_new = g4 + jnp.maximum(m_st, jnp.max(r4, axis=-1, keepdims=True))
            ws_refs[d][b] = jnp.exp(g4 + r4 - m_new)
            dec_ref[0, N_HEADS * grp:N_HEADS * (grp + 1), :] = jnp.exp(g4 + m_st - m_new)[:, :LANES]
            m_ref[grp] = m_new
            if emit:
                run = r4
                k = 1
                while k < chunk:
                    if d == 0:
                        shifted = jnp.where(lane >= k, pltpu.roll(run, k, axis=1), MASK_NEG)
                    else:
                        shifted = jnp.where(lane < chunk - k, pltpu.roll(run, chunk - k, axis=1), MASK_NEG)
                    run = jnp.maximum(run, shifted)
                    k *= 2
                m4 = jnp.maximum(m_st, run)
                rows = jnp.concatenate(
                    [m4, jnp.exp(m_st - m4), jnp.exp(-bq - m4),
                     jnp.zeros((LANES - 3 * N_HEADS, chunk), F32)], axis=0)
                cols = rows.T
                col_refs[d][b] = cols
                for hd in range(N_HEADS):
                    p = jnp.where(visible, jnp.exp(r4[hd:hd + 1, :] - cols[:, hd:hd + 1]), 0.0)
                    p_refs[d][b, :, hd * chunk:(hd + 1) * chunk] = p.astype(BF16)


def _mgate_call(gr, m0, *, chunk, emit, name):
    bsz, _, s = gr.shape
    nc = s // chunk
    ngroup = 2 * bsz
    has_init = m0 is not None
    fwd = lambda i: i
    bwd = lambda i: nc - 1 - i
    args = [gr, gr]
    specs = [pl.BlockSpec((bsz, N_GATES, chunk), lambda i: (0, 0, fwd(i))),
             pl.BlockSpec((bsz, N_GATES, chunk), lambda i: (0, 0, bwd(i)))]
    m_shape = (ngroup, N_HEADS, chunk)
    if has_init:
        args.append(m0)
        specs.append(_const_spec(m_shape))
    out_shape, out_specs = [], []
    if emit:
        out_shape += [jax.ShapeDtypeStruct((bsz, s, N_HEADS * chunk), BF16)] * 2
        out_specs += [pl.BlockSpec((bsz, chunk, N_HEADS * chunk), lambda i: (0, fwd(i), 0)),
                      pl.BlockSpec((bsz, chunk, N_HEADS * chunk), lambda i: (0, bwd(i), 0))]
        out_shape += [jax.ShapeDtypeStruct((bsz, s, LANES), F32)] * 2
        out_specs += [pl.BlockSpec((bsz, chunk, LANES), lambda i: (0, fwd(i), 0)),
                      pl.BlockSpec((bsz, chunk, LANES), lambda i: (0, bwd(i), 0))]
    out_shape += [jax.ShapeDtypeStruct((bsz, N_HEADS, s), F32)] * 2
    out_specs += [pl.BlockSpec((bsz, N_HEADS, chunk), lambda i: (0, 0, fwd(i))),
                  pl.BlockSpec((bsz, N_HEADS, chunk), lambda i: (0, 0, bwd(i)))]
    out_shape.append(jax.ShapeDtypeStruct((nc, ngroup * N_HEADS, LANES), F32))
    out_specs.append(pl.BlockSpec((1, ngroup * N_HEADS, LANES), lambda i: (i, 0, 0)))
    if emit:
        scratch = [pltpu.VMEM(m_shape, F32)]
    else:
        out_shape.append(jax.ShapeDtypeStruct(m_shape, F32))
        out_specs.append(pl.BlockSpec(m_shape, lambda i: (0, 0, 0)))
        scratch = []
    kern = functools.partial(_mgate_kernel, chunk=chunk, nb=bsz, emit=emit, has_init=has_init)
    return pl.pallas_call(
        kern,
        out_shape=out_shape,
        grid=(nc,),
        in_specs=specs,
        out_specs=out_specs,
        scratch_shapes=scratch,
        compiler_params=_params(("arbitrary",)),
        name=name,
    )(*args)


def _mlstm_kernel(*refs, chunk, nb, emit, has_init):
    it = iter(refs)
    dec_ref = next(it)
    n_in = 7 if emit else 4
    dirs = [tuple(next(it) for _ in range(n_in)) for _ in range(2)]
    init = (next(it), next(it)) if has_init else None
    h_refs = (next(it), next(it)) if emit else None
    c_ref, n_ref = next(it), next(it)

    step = pl.program_id(0)
    nchain = 2 * nb * N_HEADS

    @pl.when(step == 0)
    def _():
        if has_init:
            c_ref[...] = init[0][...]
            n_ref[...] = init[1][...]
        else:
            c_ref[...] = jnp.zeros_like(c_ref)
            n_ref[...] = jnp.zeros_like(n_ref)

    for d, dir_refs in enumerate(dirs):
        if emit:
            q_ref, kt_ref, k_ref, v_ref, p_ref, col_ref, ws_ref = dir_refs
        else:
            kt_ref, k_ref, v_ref, ws_ref = dir_refs
        for b in range(nb):
            for hd in range(N_HEADS):
                j = _chain(d, b, hd, nb)
                lanes = slice(hd * HEAD_DIM, (hd + 1) * HEAD_DIM)
                decay = dec_ref[step * nchain + j]
                ws = ws_ref[b, hd:hd + 1, :].astype(BF16)
                kt = kt_ref[b, lanes, :]
                k = k_ref[b, :, lanes]
                v = v_ref[b, :, lanes]
                ct = c_ref[j]
                n_st = n_ref[j]
                if emit:
                    q = q_ref[b, :, lanes]
                    inter = col_ref[b, :, N_HEADS + hd:N_HEADS + hd + 1]
                    ei = col_ref[b, :, 2 * N_HEADS + hd:2 * N_HEADS + hd + 1]
                    s_ts = _dot(q, kt) * p_ref[b, :, hd * chunk:(hd + 1) * chunk].astype(F32)
                    num = _dot(s_ts.astype(BF16), v) + inter * _dot(q, ct.astype(BF16))
                    qn = jnp.sum(q.astype(F32) * n_st, axis=-1, keepdims=True)
                    den = jnp.sum(s_ts, axis=-1, keepdims=True) + inter * qn
                    h = num * (1.0 / jnp.maximum(jnp.abs(den), ei))
                    h_refs[d][b, :, lanes] = h.astype(BF16)
                c_ref[j] = decay * ct + _dot(kt * ws, v)
                n_ref[j] = decay * n_st + _dot(ws, k)


def _mlstm_call(dec, q, kt, k, v, gate_f, gate_b, init, *, chunk, emit, name):
    bsz, d, s = kt.shape
    nc = s // chunk
    nchain = 2 * bsz * N_HEADS
    has_init = init is not None

    def dir_args(gate, idx):
        tok = lambda i: (0, idx(i), 0)
        feat = lambda i: (0, 0, idx(i))
        args, specs = [], []
        if emit:
            args.append(q)
            specs.append(pl.BlockSpec((bsz, chunk, d), tok))
        args += [kt, k, v]
        specs += [pl.BlockSpec((bsz, d, chunk), feat),
                  pl.BlockSpec((bsz, chunk, d), tok), pl.BlockSpec((bsz, chunk, d), tok)]
        if emit:
            p, col, ws = gate
            args += [p, col]
            specs += [pl.BlockSpec((bsz, chunk, N_HEADS * chunk), tok),
                      pl.BlockSpec((bsz, chunk, LANES), tok)]
        else:
            (ws,) = gate
        args.append(ws)
        specs.append(pl.BlockSpec((bsz, N_HEADS, chunk), feat))
        return args, specs

    fwd = lambda i: i
    bwd = lambda i: nc - 1 - i
    af, sf = dir_args(gate_f, fwd)
    ab, sb = dir_args(gate_b, bwd)
    args = [dec] + af + ab
    specs = [pl.BlockSpec(memory_space=pltpu.SMEM)] + sf + sb
    state_shapes = [(nchain, HEAD_DIM, HEAD_DIM), (nchain, 1, HEAD_DIM)]
    if has_init:
        args += list(init)
        specs += [_const_spec(sh) for sh in state_shapes]
    kern = functools.partial(_mlstm_kernel, chunk=chunk, nb=bsz, emit=emit, has_init=has_init)
    if emit:
        out_shape = [jax.ShapeDtypeStruct((bsz, s, d), BF16)] * 2
        out_specs = [pl.BlockSpec((bsz, chunk, d), lambda i: (0, fwd(i), 0)),
                     pl.BlockSpec((bsz, chunk, d), lambda i: (0, bwd(i), 0))]
        scratch = [pltpu.VMEM(sh, F32) for sh in state_shapes]
    else:
        out_shape = [jax.ShapeDtypeStruct(sh, F32) for sh in state_shapes]
        out_specs = [pl.BlockSpec(sh, lambda i: (0, 0, 0)) for sh in state_shapes]
        scratch = []
    return pl.pallas_call(
        kern,
        out_shape=out_shape,
        grid=(nc,),
        in_specs=specs,
        out_specs=out_specs,
        scratch_shapes=scratch,
        compiler_params=_params(("arbitrary",)),
        name=name,
    )(*args)


def _merge_kernel(hf_ref, hb_ref, so_ref, gu_ref, gv_ref, sa_ref, sb_ref, h_ref, mods_ref,
                  ghead_ref, gsgu_ref, ws_ref, bs_ref, wa_ref, wb_ref, wo_ref, o_ref, yb_ref, *, tm):
    hm = hf_ref[0].astype(F32) + hb_ref[0].astype(F32)
    parts = []
    for hd in range(N_HEADS):
        x = hm[:, hd * HEAD_DIM:(hd + 1) * HEAD_DIM]
        xc = x - jnp.mean(x, axis=-1, keepdims=True)
        parts.append(xc * lax.rsqrt(jnp.mean(xc * xc, axis=-1, keepdims=True) + EPS))
    ln = jnp.concatenate(parts, axis=-1) * ghead_ref[...]
    y_a = (so_ref[0].astype(F32) * ln).astype(BF16)

    vn = _rms(gv_ref[0].astype(F32), gsgu_ref[...]).astype(BF16)
    for r in range(tm // SGU_CHUNK):
        rows = slice(r * SGU_CHUNK, (r + 1) * SGU_CHUNK)
        for gidx in range(N_GROUPS):
            cols = slice(gidx * GROUP_DIM, (gidx + 1) * GROUP_DIM)
            mixed = _dot(ws_ref[gidx], vn[rows, cols]) + bs_ref[:, gidx:gidx + 1]
            yb_ref[rows, cols] = (gu_ref[0, rows, cols].astype(F32) * mixed).astype(BF16)

    mixed_out = (sa_ref[0].astype(F32) * _dot(y_a, wa_ref[...])
                 + sb_ref[0].astype(F32) * _dot(yb_ref[...], wb_ref[...]))
    y = _dot(mixed_out.astype(BF16), wo_ref[...])
    o_ref[0] = h_ref[0] + mods_ref[0][5:6] * y


def _merge_call(hf, hb, so, gu, gv, sa, sb, h, mods, g_head, g_sgu, w_s, b_s_t, w_a, w_b, w_o, *, tm):
    bsz, s, d = h.shape
    tok = pl.BlockSpec((1, tm, d), lambda i, b: (b, i, 0))
    specs = [tok] * 8 + [
        pl.BlockSpec((1, N_MOD, d), lambda i, b: (b, 0, 0)),
        _const_spec((1, d)), _const_spec((1, d)),
        _const_spec(w_s.shape), _const_spec(b_s_t.shape),
        _const_spec(w_a.shape), _const_spec(w_b.shape), _const_spec(w_o.shape),
    ]
    return pl.pallas_call(
        functools.partial(_merge_kernel, tm=tm),
        out_shape=jax.ShapeDtypeStruct((bsz, s, d), F32),
        grid=(s // tm, bsz),
        in_specs=specs,
        out_specs=tok,
        scratch_shapes=[pltpu.VMEM((tm, d), BF16)],
        compiler_params=_params(("arbitrary", "arbitrary")),
        name="merge",
    )(hf, hb, so, gu, gv, sa, sb, h, mods, g_head.reshape(1, d), g_sgu.reshape(1, d),
      w_s, b_s_t, w_a, w_b, w_o)


def _pos_tables(rows):
    quarter = D_MODEL // 4
    freqs = jnp.exp(-math.log(POS_BASE) * jnp.arange(quarter, dtype=F32) / quarter)
    ar = jnp.arange(rows, dtype=F32)[:, None] * freqs
    ac = jnp.arange(GRID_W, dtype=F32)[:, None] * freqs
    return (jnp.concatenate([jnp.sin(ar), jnp.cos(ar)], axis=-1),
            jnp.concatenate([jnp.sin(ac), jnp.cos(ac)], axis=-1))


def kernel(x, c, ctx, c_ctx, w_ada, b_ada, g_ffn1, w_ffn1_in, w_ffn1_out, g_mix, w_in, b_gates,
           conv_qk_w, conv_qk_b, g_head, g_sgu, w_s, b_s, w_branch_a, w_branch_b, w_out,
           g_ffn2, w_ffn2_in, w_ffn2_out, g_final):
    bsz, seq, d = x.shape
    ctx_len = ctx.shape[1]
    layer = 0
    pos = _pos_tables(seq // GRID_W)

    cond = jnp.concatenate([c, c_ctx[None], jnp.zeros((HALO - bsz - 1, d), F32)], axis=0)
    mods = _mods_call(cond, w_ada[layer], b_ada[layer]).reshape(HALO, N_MOD, d)
    lat_row = lambda b: b
    ctx_row = lambda b: bsz

    w1_in = w_ffn1_in[layer].astype(BF16)
    w1_out = w_ffn1_out[layer].astype(BF16)
    w2_in = w_ffn2_in[layer].astype(BF16)
    w2_out = w_ffn2_out[layer].astype(BF16)
    wi = w_in[layer]
    wqk = wi[:, :2 * d].astype(BF16)
    wv = wi[:, 2 * d:3 * d].astype(BF16)
    wgt = wi[:, 3 * d:3 * d + N_GATES].T.astype(BF16)
    wrest = wi[:, 3 * d + N_GATES:].astype(BF16)
    bg = b_gates[layer]

    def bg_row(tm):
        return jnp.broadcast_to(bg[:, None], (N_GATES, tm))

    def dec_table(dec):
        return dec[:, :, 0].reshape(-1)

    h = _ffn_call(x, pos, mods, lat_row, g_ffn1[layer], w1_in, w1_out, None,
                  mod_base=0, tm=FFN_TM, name="ffn1")
    hc = _ffn_call(ctx, None, mods, ctx_row, g_ffn1[layer], w1_in, w1_out, None,
                   mod_base=0, tm=ctx_len, name="ffn1_ctx")

    _, k_c, kt_c, v_c, gr_c = _proj_call(
        hc, mods, ctx_row, g_mix[layer], wqk, wv, wgt, None, conv_qk_w[layer], conv_qk_b[layer],
        bg_row(ctx_len), tm=ctx_len, name="proj_ctx")
    q_l, k_l, kt_l, v_l, gr_l, so, gu, gv, sa, sb = _proj_call(
        h, mods, lat_row, g_mix[layer], wqk, wv, wgt, wrest, conv_qk_w[layer], conv_qk_b[layer],
        bg_row(PROJ_TM), tm=PROJ_TM, name="proj")

    ws_cf, ws_cb, dec_c, m_c = _mgate_call(gr_c, None, chunk=MLSTM_L, emit=False, name="mgate_ctx")
    p_f, p_b, col_f, col_b, ws_f, ws_b, dec_l = _mgate_call(
        gr_l, m_c, chunk=MLSTM_L, emit=True, name="mgate")
    state = _mlstm_call(dec_table(dec_c), None, kt_c, k_c, v_c, (ws_cf,), (ws_cb,), None,
                        chunk=MLSTM_L, emit=False, name="mlstm_ctx")
    hf, hb = _mlstm_call(dec_table(dec_l), q_l, kt_l, k_l, v_l, (p_f, col_f, ws_f), (p_b, col_b, ws_b),
                         state, chunk=MLSTM_L, emit=True, name="mlstm")

    h = _merge_call(hf, hb, so, gu, gv, sa, sb, h, mods, g_head[layer], g_sgu[layer],
                    w_s[layer].astype(BF16), b_s[layer].T, w_branch_a[layer].astype(BF16),
                    w_branch_b[layer].astype(BF16), w_out[layer].astype(BF16), tm=MERGE_TM)

    return _ffn_call(h, None, mods, lat_row, g_ffn2[layer], w2_in, w2_out, g_final,
                     mod_base=6, tm=FFN_TM, name="ffn2")
```

```python
import functools
import math

import jax
import jax.numpy as jnp
from jax import lax
from jax.experimental import pallas as pl
from jax.experimental.pallas import tpu as pltpu

F32 = jnp.float32
BF16 = jnp.bfloat16

D_MODEL = 1024
GRID_W = 64
N_HEADS = 4
HEAD_DIM = 256
N_GROUPS = 4
GROUP_DIM = 256
SGU_CHUNK = 128
D_FF = 2816
N_MOD = 9
N_GATES = 16
POS_BASE = 10000.0
EPS = 1e-6

LANES = 128
HALO = 8
VMEM_LIMIT = 56 * 1024 * 1024
MASK_NEG = -1e30

FFN_TM = 512
PROJ_TM = 512
TAIL_TM = 256
MLSTM_L = 256
FF_CHUNKS = ((0, 1024), (1024, 2048), (2048, D_FF))


def _dot(a, b):
    return jnp.dot(a, b, preferred_element_type=F32)


def _sigmoid(x):
    return 1.0 / (1.0 + jnp.exp(-x))


def _gelu_tanh(x):
    c = math.sqrt(2.0 / math.pi)
    return 0.5 * x * (1.0 + jnp.tanh(c * (x + 0.044715 * (x * x * x))))


def _log_sigmoid(x):
    return jnp.minimum(x, 0.0) - jnp.log(1.0 + jnp.exp(-jnp.abs(x)))


def _rms(x, g):
    return x * lax.rsqrt(jnp.mean(x * x, axis=-1, keepdims=True) + EPS) * g


def _norm_mod(x, g, shift, scale):
    return _rms(x, g) * (1.0 + scale) + shift


def _const_spec(shape):
    zeros = (0,) * len(shape)
    return pl.BlockSpec(shape, lambda *_: zeros, pipeline_mode=pl.Buffered(1))


def _params(sem):
    return pltpu.CompilerParams(dimension_semantics=sem, vmem_limit_bytes=VMEM_LIMIT)


def _mods_kernel(c_ref, w_ref, b_ref, o_ref):
    c = c_ref[...]
    s = (c * _sigmoid(c)).astype(BF16)
    o_ref[...] = _dot(s, w_ref[...].astype(BF16)) + b_ref[...]


def _mods_call(cond, w_ada, b_ada):
    rows, d = cond.shape
    n = w_ada.shape[1]
    tn = 1024
    return pl.pallas_call(
        _mods_kernel,
        out_shape=jax.ShapeDtypeStruct((rows, n), F32),
        grid=(n // tn,),
        in_specs=[
            pl.BlockSpec((rows, d), lambda j: (0, 0)),
            pl.BlockSpec((d, tn), lambda j: (0, j)),
            pl.BlockSpec((1, tn), lambda j: (0, j)),
        ],
        out_specs=pl.BlockSpec((rows, tn), lambda j: (0, j)),
        compiler_params=_params(("arbitrary",)),
        name="mods",
    )(cond, w_ada, b_ada.reshape(1, n))


def _ffn_body(x, m, mod_base, g, win_ref, wout_ref):
    shift = m[mod_base:mod_base + 1]
    scale = m[mod_base + 1:mod_base + 2]
    gate = m[mod_base + 2:mod_base + 3]
    hb = _norm_mod(x, g, shift, scale).astype(BF16)
    y = None
    for c0, c1 in FF_CHUNKS:
        a = _dot(hb, win_ref[:, c0:c1])
        b = _dot(hb, win_ref[:, D_FF + c0:D_FF + c1])
        act = (a * _sigmoid(a) * b).astype(BF16)
        part = _dot(act, wout_ref[c0:c1, :])
        y = part if y is None else y + part
    return x + 0.5 * gate * y


def _ffn_kernel(*refs, tm, mod_base, add_pos):
    it = iter(refs)
    x_ref = next(it)
    prow_ref, pcol_ref = (next(it), next(it)) if add_pos else (None, None)
    mods_ref, g_ref, win_ref, wout_ref, o_ref = (next(it) for _ in range(5))

    x = x_ref[0]
    if add_pos:
        pcol = pcol_ref[...]
        half = pcol.shape[1]
        pos = [jnp.concatenate([jnp.broadcast_to(prow_ref[r:r + 1, :], (GRID_W, half)), pcol], axis=-1)
               for r in range(tm // GRID_W)]
        x = x + jnp.concatenate(pos, axis=0)
    o_ref[0] = _ffn_body(x, mods_ref[0], mod_base, g_ref[...], win_ref, wout_ref)


def _ffn_call(x, pos, mods, mod_row, g, w_in, w_out, *, mod_base, tm, name):
    bsz, s, d = x.shape
    add_pos = pos is not None
    args = [x]
    specs = [pl.BlockSpec((1, tm, d), lambda i, b: (b, i, 0))]
    if add_pos:
        prow, pcol = pos
        args += [prow, pcol]
        specs += [pl.BlockSpec((tm // GRID_W, prow.shape[1]), lambda i, b: (i, 0)),
                  _const_spec(pcol.shape)]
    args += [mods, g.reshape(1, d), w_in, w_out]
    specs += [
        pl.BlockSpec((1, N_MOD, d), lambda i, b: (mod_row(b), 0, 0)),
        _const_spec((1, d)),
        _const_spec(w_in.shape),
        _const_spec(w_out.shape),
    ]
    kern = functools.partial(_ffn_kernel, tm=tm, mod_base=mod_base, add_pos=add_pos)
    return pl.pallas_call(
        kern,
        out_shape=jax.ShapeDtypeStruct((bsz, s, d), F32),
        grid=(s // tm, bsz),
        in_specs=specs,
        out_specs=pl.BlockSpec((1, tm, d), lambda i, b: (b, i, 0)),
        compiler_params=_params(("arbitrary", "arbitrary")),
        name=name,
    )(*args)


def _proj_kernel(*refs, tm, with_rest):
    it = iter(refs)
    h_ref, hp_ref, hx_ref, mods_ref, g_ref = (next(it) for _ in range(5))
    wqk_ref, wv_ref, wgt_ref = (next(it) for _ in range(3))
    wrest_ref = next(it) if with_rest else None
    cw_ref, cb_ref, bgr_ref = (next(it) for _ in range(3))
    q_ref, k_ref, kt_ref, v_ref, gr_ref = (next(it) for _ in range(5))
    rest_refs = [next(it) for _ in range(5)] if with_rest else []
    raw_ref = next(it)

    i = pl.program_id(0)
    nt = pl.num_programs(0)
    m = mods_ref[0]
    shift, scale = m[3:4], m[4:5]
    g = g_ref[...]
    hn = _norm_mod(h_ref[0], g, shift, scale)
    prev_ok = (i > 0).astype(F32)
    next_ok = (i < nt - 1).astype(F32)
    hn_prev = _norm_mod(hp_ref[0], g, shift, scale) * prev_ok
    hn_next = _norm_mod(hx_ref[0], g, shift, scale) * next_ok
    hb = hn.astype(BF16)
    ext = jnp.concatenate([hn_prev, hn, hn_next], axis=0).astype(BF16)

    raw_ref[...] = _dot(ext, wqk_ref[...])
    sub8 = lax.broadcasted_iota(jnp.int32, (HALO, HEAD_DIM), 0)

    def conv_chunk(c):
        cols = slice(c * HEAD_DIM, (c + 1) * HEAD_DIM)
        cw = cw_ref[:, cols]
        x0 = raw_ref[HALO:HALO + tm, cols]
        down = pltpu.roll(x0, 1, axis=0)
        first = jnp.where(sub8 == 0, raw_ref[HALO - 1:HALO, cols], down[:HALO])
        xm = jnp.concatenate([first, down[HALO:]], axis=0)
        up = pltpu.roll(x0, tm - 1, axis=0)
        last = jnp.where(sub8 == HALO - 1, raw_ref[HALO + tm:HALO + tm + 1, cols], up[tm - HALO:])
        xp = jnp.concatenate([up[:tm - HALO], last], axis=0)
        acc = xm * cw[0:1] + x0 * cw[1:2] + xp * cw[2:3] + cb_ref[:, cols]
        act = acc * _sigmoid(acc)
        if c < N_HEADS:
            q_ref[0, :, cols] = (act * HEAD_DIM ** -0.5).astype(BF16)
        else:
            kc = slice((c - N_HEADS) * HEAD_DIM, (c - N_HEADS + 1) * HEAD_DIM)
            k_ref[0, :, kc] = act.astype(BF16)
            kt_ref[0, kc, :] = act.T.astype(BF16)

    def rest_piece(j):
        rest_refs[j][0] = _dot(hb, wrest_ref[:, j * D_MODEL:(j + 1) * D_MODEL]).astype(BF16)

    grow = lax.dot_general(wgt_ref[...], hb, (((1,), (1,)), ((), ())),
                           preferred_element_type=F32) + bgr_ref[...]
    sub = lax.broadcasted_iota(jnp.int32, grow.shape, 0)
    gr_ref[0] = jnp.where(sub % 8 >= N_HEADS, _log_sigmoid(grow), grow)

    v_ref[0] = _dot(hb, wv_ref[...]).astype(BF16)
    for c in range(2 * N_HEADS):
        if with_rest and c < 5:
            rest_piece(c)
        conv_chunk(c)


def _proj_call(h, mods, mod_row, g, wqk, wv, wgt, wrest, conv_w, conv_b, bg_row, *, tm, name):
    bsz, s, d = h.shape
    with_rest = wrest is not None
    nhb = tm // HALO
    last = s // HALO - 1
    args = [h, h, h, mods, g.reshape(1, d), wqk, wv, wgt]
    specs = [
        pl.BlockSpec((1, tm, d), lambda i, b: (b, i, 0)),
        pl.BlockSpec((1, HALO, d), lambda i, b: (b, jnp.maximum(i * nhb - 1, 0), 0)),
        pl.BlockSpec((1, HALO, d), lambda i, b: (b, jnp.minimum((i + 1) * nhb, last), 0)),
        pl.BlockSpec((1, N_MOD, d), lambda i, b: (mod_row(b), 0, 0)),
        _const_spec((1, d)),
        _const_spec(wqk.shape), _const_spec(wv.shape), _const_spec(wgt.shape),
    ]
    if with_rest:
        args.append(wrest)
        specs.append(_const_spec(wrest.shape))
    args += [conv_w, conv_b.reshape(1, -1), bg_row]
    specs += [_const_spec(conv_w.shape), _const_spec((1, conv_b.shape[0])), _const_spec(bg_row.shape)]
    tok = lambda i, b: (b, i, 0)
    out_shape = [
        jax.ShapeDtypeStruct((bsz, s, d), BF16),
        jax.ShapeDtypeStruct((bsz, s, d), BF16),
        jax.ShapeDtypeStruct((bsz, d, s), BF16),
        jax.ShapeDtypeStruct((bsz, s, d), BF16),
        jax.ShapeDtypeStruct((bsz, N_GATES, s), F32),
    ]
    out_specs = [
        pl.BlockSpec((1, tm, d), tok), pl.BlockSpec((1, tm, d), tok),
        pl.BlockSpec((1, d, tm), lambda i, b: (b, 0, i)),
        pl.BlockSpec((1, tm, d), tok),
        pl.BlockSpec((1, N_GATES, tm), lambda i, b: (b, 0, i)),
    ]
    if with_rest:
        out_shape += [jax.ShapeDtypeStruct((bsz, s, d), BF16)] * 5
        out_specs += [pl.BlockSpec((1, tm, d), tok)] * 5
    kern = functools.partial(_proj_kernel, tm=tm, with_rest=with_rest)
    return pl.pallas_call(
        kern,
        out_shape=out_shape,
        grid=(s // tm, bsz),
        in_specs=specs,
        out_specs=out_specs,
        scratch_shapes=[pltpu.VMEM((tm + 2 * HALO, 2 * d), F32)],
        compiler_params=_params(("arbitrary", "arbitrary")),
        name=name,
    )(*args)


def _split3(x):
    hi = x.astype(BF16)
    r1 = x - hi.astype(F32)
    mid = r1.astype(BF16)
    lo = (r1 - mid.astype(F32)).astype(BF16)
    return hi, mid, lo


def _chain(d, b, hd, nb):
    return (d * nb + b) * N_HEADS + hd


def _mgate_kernel(*refs, chunk, nb, emit, has_init):
    it = iter(refs)
    gr_refs = (next(it), next(it))
    m0_ref = next(it) if has_init else None
    p_refs = (next(it), next(it)) if emit else None
    rep_refs = (next(it), next(it)) if emit else None
    ws_refs = (next(it), next(it))
    dec_ref = next(it)
    m_ref = next(it)

    @pl.when(pl.program_id(0) == 0)
    def _():
        m_ref[...] = m0_ref[...] if has_init else jnp.zeros_like(m_ref)

    row = lax.broadcasted_iota(jnp.int32, (chunk, chunk), 0)
    col = lax.broadcasted_iota(jnp.int32, (chunk, chunk), 1)
    lane = lax.broadcasted_iota(jnp.int32, (nb * N_HEADS, chunk), 1)
    ones = jnp.ones((chunk, chunk), BF16)
    sel_r = lax.broadcasted_iota(jnp.int32, (LANES, 2 * N_HEADS * LANES), 0)
    sel_j = lax.broadcasted_iota(jnp.int32, (LANES, 2 * N_HEADS * LANES), 1) // LANES
    spread = jnp.where(sel_r == sel_j + N_HEADS, 1.0, 0.0).astype(BF16)

    visible = (row >= col, row <= col)
    tri = [jnp.where(row <= col, 1.0, 0.0).astype(BF16), jnp.where(row >= col, 1.0, 0.0).astype(BF16)]

    li, bq, g4 = [], [], []
    for d, gr_ref in enumerate(gr_refs):
        parts = []
        for b in range(nb):
            gr = gr_ref[b]
            pieces = _split3(gr)
            b_all = sum(_dot(x, tri[d]) for x in pieces)
            g_all = sum(_dot(x, ones) for x in pieces)
            f_rows = slice(8 * d + N_HEADS, 8 * d + 2 * N_HEADS)
            parts.append((gr[8 * d:8 * d + N_HEADS], b_all[f_rows], g_all[f_rows]))
        li.append(jnp.concatenate([p[0] for p in parts], axis=0))
        bq.append(jnp.concatenate([p[1] for p in parts], axis=0))
        g4.append(jnp.concatenate([p[2] for p in parts], axis=0))

    r, m_st = [], []
    for d in range(2):
        r.append(li[d] - bq[d])
        m_st.append(m_ref[d])
        m_new = g4[d] + jnp.maximum(m_st[d], jnp.max(r[d], axis=-1, keepdims=True))
        ws = jnp.exp(g4[d] + r[d] - m_new)
        for b in range(nb):
            ws_refs[d][b] = ws[b * N_HEADS:(b + 1) * N_HEADS]
        dec_ref[0, d * nb * N_HEADS:(d + 1) * nb * N_HEADS, :] = jnp.exp(g4[d] + m_st[d] - m_new)[:, :LANES]
        m_ref[d] = m_new
    if not emit:
        return

    run = list(r)
    k = 1
    while k < chunk:
        for d in range(2):
            if d == 0:
                shifted = jnp.where(lane >= k, pltpu.roll(run[d], k, axis=1), MASK_NEG)
            else:
                shifted = jnp.where(lane < chunk - k, pltpu.roll(run[d], chunk - k, axis=1), MASK_NEG)
            run[d] = jnp.maximum(run[d], shifted)
        k *= 2

    cols = {}
    for d in range(2):
        m4 = jnp.maximum(m_st[d], run[d])
        inter = jnp.exp(m_st[d] - m4)
        ei = jnp.exp(-bq[d] - m4)
        for b in range(nb):
            sl = slice(b * N_HEADS, (b + 1) * N_HEADS)
            rows = jnp.concatenate([m4[sl], inter[sl], ei[sl],
                                    jnp.zeros((LANES - 3 * N_HEADS, chunk), F32)], axis=0)
            cols[d, b] = rows.T
    for (d, b), c in cols.items():
        rep_refs[d][b] = _dot(c.astype(BF16), spread).astype(BF16)
    for (d, b), c in cols.items():
        for hd in range(N_HEADS):
            src_row = r[d][b * N_HEADS + hd:b * N_HEADS + hd + 1, :]
            p = jnp.where(visible[d], jnp.exp(src_row - c[:, hd:hd + 1]), 0.0)
            p_refs[d][b, :, hd * chunk:(hd + 1) * chunk] = p.astype(BF16)


def _mgate_call(gr, m0, *, chunk, emit, name):
    bsz, _, s = gr.shape
    nc = s // chunk
    ngroup = 2 * bsz
    has_init = m0 is not None
    fwd = lambda i: i
    bwd = lambda i: nc - 1 - i
    args = [gr, gr]
    specs = [pl.BlockSpec((bsz, N_GATES, chunk), lambda i: (0, 0, fwd(i))),
             pl.BlockSpec((bsz, N_GATES, chunk), lambda i: (0, 0, bwd(i)))]
    m_shape = (2, bsz * N_HEADS, chunk)
    if has_init:
        args.append(m0)
        specs.append(_const_spec(m_shape))
    out_shape, out_specs = [], []
    if emit:
        out_shape += [jax.ShapeDtypeStruct((bsz, s, N_HEADS * chunk), BF16)] * 2
        out_specs += [pl.BlockSpec((bsz, chunk, N_HEADS * chunk), lambda i: (0, fwd(i), 0)),
                      pl.BlockSpec((bsz, chunk, N_HEADS * chunk), lambda i: (0, bwd(i), 0))]
        out_shape += [jax.ShapeDtypeStruct((bsz, s, 2 * N_HEADS * LANES), BF16)] * 2
        out_specs += [pl.BlockSpec((bsz, chunk, 2 * N_HEADS * LANES), lambda i: (0, fwd(i), 0)),
                      pl.BlockSpec((bsz, chunk, 2 * N_HEADS * LANES), lambda i: (0, bwd(i), 0))]
    out_shape += [jax.ShapeDtypeStruct((bsz, N_HEADS, s), F32)] * 2
    out_specs += [pl.BlockSpec((bsz, N_HEADS, chunk), lambda i: (0, 0, fwd(i))),
                  pl.BlockSpec((bsz, N_HEADS, chunk), lambda i: (0, 0, bwd(i)))]
    out_shape.append(jax.ShapeDtypeStruct((nc, ngroup * N_HEADS, LANES), F32))
    out_specs.append(pl.BlockSpec((1, ngroup * N_HEADS, LANES), lambda i: (i, 0, 0)))
    if emit:
        scratch = [pltpu.VMEM(m_shape, F32)]
    else:
        out_shape.append(jax.ShapeDtypeStruct(m_shape, F32))
        out_specs.append(pl.BlockSpec(m_shape, lambda i: (0, 0, 0)))
        scratch = []
    kern = functools.partial(_mgate_kernel, chunk=chunk, nb=bsz, emit=emit, has_init=has_init)
    return pl.pallas_call(
        kern,
        out_shape=out_shape,
        grid=(nc,),
        in_specs=specs,
        out_specs=out_specs,
        scratch_shapes=scratch,
        compiler_params=_params(("arbitrary",)),
        name=name,
    )(*args)


def _fold_lanes(x):
    blocks = [x[:, c:c + LANES] for c in range(0, x.shape[1], LANES)]
    return functools.reduce(lambda a, b: a + b, blocks)


def _tile_lanes(x, width):
    return jnp.concatenate([x] * (width // LANES), axis=-1)


def _mlstm_kernel(*refs, chunk, nb, emit, has_init):
    it = iter(refs)
    dec_ref = next(it)
    n_in = 7 if emit else 4
    dirs = [tuple(next(it) for _ in range(n_in)) for _ in range(2)]
    init = (next(it), next(it)) if has_init else None
    h_refs = (next(it), next(it)) if emit else None
    c_ref, n_ref = next(it), next(it)

    step = pl.program_id(0)
    nchain = 2 * nb * N_HEADS

    @pl.when(step == 0)
    def _():
        if has_init:
            c_ref[...] = init[0][...]
            n_ref[...] = init[1][...]
        else:
            c_ref[...] = jnp.zeros_like(c_ref)
            n_ref[...] = jnp.zeros_like(n_ref)

    for d, dir_refs in enumerate(dirs):
        if emit:
            q_ref, kt_ref, k_ref, v_ref, p_ref, rep_ref, ws_ref = dir_refs
        else:
            kt_ref, k_ref, v_ref, ws_ref = dir_refs
        for b in range(nb):
            for hd in range(N_HEADS):
                j = _chain(d, b, hd, nb)
                lanes = slice(hd * HEAD_DIM, (hd + 1) * HEAD_DIM)
                decay = dec_ref[step * nchain + j]
                ws = ws_ref[b, hd:hd + 1, :].astype(BF16)
                kt = kt_ref[b, lanes, :]
                k = k_ref[b, :, lanes]
                v = v_ref[b, :, lanes]
                ct = c_ref[j]
                n_st = n_ref[j]
                if emit:
                    q = q_ref[b, :, lanes]
                    inter = rep_ref[b, :, hd * LANES:(hd + 1) * LANES].astype(F32)
                    ei = rep_ref[b, :, (N_HEADS + hd) * LANES:(N_HEADS + hd + 1) * LANES].astype(F32)
                    s_ts = _dot(q, kt) * p_ref[b, :, hd * chunk:(hd + 1) * chunk].astype(F32)
                    num_intra = _dot(s_ts.astype(BF16), v)
                    num_inter = _dot(q, ct.astype(BF16))
                    den = jnp.sum(_fold_lanes(s_ts) + inter * _fold_lanes(q.astype(F32) * n_st),
                                  axis=-1, keepdims=True)
                    rden = 1.0 / jnp.maximum(jnp.abs(den), ei)
                    h = (num_intra + _tile_lanes(inter, HEAD_DIM) * num_inter) * _tile_lanes(rden, HEAD_DIM)
                    h_refs[d][b, :, lanes] = h.astype(BF16)
                c_ref[j] = decay * ct + _dot(kt * ws, v)
                n_ref[j] = decay * n_st + _dot(ws, k)


def _mlstm_call(dec, q, kt, k, v, gate_f, gate_b, init, *, chunk, emit, name):
    bsz, d, s = kt.shape
    nc = s // chunk
    nchain = 2 * bsz * N_HEADS
    has_init = init is not None

    def dir_args(gate, idx):
        tok = lambda i: (0, idx(i), 0)
        feat = lambda i: (0, 0, idx(i))
        args, specs = [], []
        if emit:
            args.append(q)
            specs.append(pl.BlockSpec((bsz, chunk, d), tok))
        args += [kt, k, v]
        specs += [pl.BlockSpec((bsz, d, chunk), feat),
                  pl.BlockSpec((bsz, chunk, d), tok), pl.BlockSpec((bsz, chunk, d), tok)]
        if emit:
            p, rep, ws = gate
            args += [p, rep]
            specs += [pl.BlockSpec((bsz, chunk, N_HEADS * chunk), tok),
                      pl.BlockSpec((bsz, chunk, 2 * N_HEADS * LANES), tok)]
        else:
            (ws,) = gate
        args.append(ws)
        specs.append(pl.BlockSpec((bsz, N_HEADS, chunk), feat))
        return args, specs

    fwd = lambda i: i
    bwd = lambda i: nc - 1 - i
    af, sf = dir_args(gate_f, fwd)
    ab, sb = dir_args(gate_b, bwd)
    args = [dec] + af + ab
    specs = [pl.BlockSpec(memory_space=pltpu.SMEM)] + sf + sb
    state_shapes = [(nchain, HEAD_DIM, HEAD_DIM), (nchain, 1, HEAD_DIM)]
    if has_init:
        args += list(init)
        specs += [_const_spec(sh) for sh in state_shapes]
    kern = functools.partial(_mlstm_kernel, chunk=chunk, nb=bsz, emit=emit, has_init=has_init)
    if emit:
        out_shape = [jax.ShapeDtypeStruct((bsz, s, d), BF16)] * 2
        out_specs = [pl.BlockSpec((bsz, chunk, d), lambda i: (0, fwd(i), 0)),
                     pl.BlockSpec((bsz, chunk, d), lambda i: (0, bwd(i), 0))]
        scratch = [pltpu.VMEM(sh, F32) for sh in state_shapes]
    else:
        out_shape = [jax.ShapeDtypeStruct(sh, F32) for sh in state_shapes]
        out_specs = [pl.BlockSpec(sh, lambda i: (0, 0, 0)) for sh in state_shapes]
        scratch = []
    return pl.pallas_call(
        kern,
        out_shape=out_shape,
        grid=(nc,),
        in_specs=specs,
        out_specs=out_specs,
        scratch_shapes=scratch,
        compiler_params=_params(("arbitrary",)),
        name=name,
    )(*args)


def _tail_kernel(hf_ref, hb_ref, o_ref, u_ref, vs_ref, ga_ref, gb_ref, h_ref, mods_ref, mods_ffn_ref,
                 ghead_ref, gsgu_ref, ws_ref, bs_ref, wa_ref, wb_ref, wo_ref,
                 g2_ref, win_ref, wout_ref, gfin_ref, out_ref, yb_ref, h2_ref, *, tm):
    @pl.when(pl.program_id(0) == 0)
    def _():
        h2_ref[...] = jnp.zeros_like(h2_ref)

    h2_prev = h2_ref[...]
    out_ref[0] = _rms(_ffn_body(h2_prev, mods_ffn_ref[0], 6, g2_ref[...], win_ref, wout_ref), gfin_ref[...])

    m = mods_ref[0]
    hm = hf_ref[0].astype(F32) + hb_ref[0].astype(F32)
    parts = []
    for hd in range(N_HEADS):
        x = hm[:, hd * HEAD_DIM:(hd + 1) * HEAD_DIM]
        xc = x - jnp.mean(x, axis=-1, keepdims=True)
        parts.append(xc * lax.rsqrt(jnp.mean(xc * xc, axis=-1, keepdims=True) + EPS))
    ln = jnp.concatenate(parts, axis=-1) * ghead_ref[...]
    y_a = (_sigmoid(o_ref[0].astype(F32)) * ln).astype(BF16)

    vn = _rms(_gelu_tanh(vs_ref[0].astype(F32)), gsgu_ref[...]).astype(BF16)
    for r in range(tm // SGU_CHUNK):
        rows = slice(r * SGU_CHUNK, (r + 1) * SGU_CHUNK)
        for gidx in range(N_GROUPS):
            cols = slice(gidx * GROUP_DIM, (gidx + 1) * GROUP_DIM)
            mixed = _dot(ws_ref[gidx], vn[rows, cols]) + bs_ref[:, gidx:gidx + 1]
            gu = _gelu_tanh(u_ref[0, rows, cols].astype(F32))
            yb_ref[rows, cols] = (gu * mixed).astype(BF16)

    mixed_out = (_sigmoid(ga_ref[0].astype(F32)) * _dot(y_a, wa_ref[...])
                 + _sigmoid(gb_ref[0].astype(F32)) * _dot(yb_ref[...], wb_ref[...]))
    y = _dot(mixed_out.astype(BF16), wo_ref[...])
    h2_ref[...] = h_ref[0] + m[5:6] * y


def _tail_call(hf, hb, o, u, vs, ga, gb, h, mods, g_head, g_sgu, w_s, b_s_t, w_a, w_b, w_o,
               g2, w_in, w_out, g_final, *, tm):
    bsz, s, d = h.shape
    ntile = bsz * (s // tm)
    merge_tile = lambda t: jnp.minimum(t, ntile - 1)
    ffn_tile = lambda t: jnp.maximum(t - 1, 0)
    tok = pl.BlockSpec((1, tm, d), lambda t: (merge_tile(t) % bsz, merge_tile(t) // bsz, 0))
    vec = _const_spec((1, d))
    specs = [tok] * 8 + [
        pl.BlockSpec((1, N_MOD, d), lambda t: (merge_tile(t) % bsz, 0, 0)),
        pl.BlockSpec((1, N_MOD, d), lambda t: (ffn_tile(t) % bsz, 0, 0)),
        vec, vec, _const_spec(w_s.shape), _const_spec(b_s_t.shape),
        _const_spec(w_a.shape), _const_spec(w_b.shape), _const_spec(w_o.shape),
        vec, _const_spec(w_in.shape), _const_spec(w_out.shape), vec,
    ]
    return pl.pallas_call(
        functools.partial(_tail_kernel, tm=tm),
        out_shape=jax.ShapeDtypeStruct((bsz, s, d), F32),
        grid=(ntile + 1,),
        in_specs=specs,
        out_specs=pl.BlockSpec((1, tm, d), lambda t: (ffn_tile(t) % bsz, ffn_tile(t) // bsz, 0)),
        scratch_shapes=[pltpu.VMEM((tm, d), BF16), pltpu.VMEM((tm, d), F32)],
        compiler_params=_params(("arbitrary",)),
        name="tail",
    )(hf, hb, o, u, vs, ga, gb, h, mods, mods, g_head.reshape(1, d), g_sgu.reshape(1, d),
      w_s, b_s_t, w_a, w_b, w_o, g2.reshape(1, d), w_in, w_out, g_final.reshape(1, d))


def _pos_tables(rows):
    quarter = D_MODEL // 4
    freqs = jnp.exp(-math.log(POS_BASE) * jnp.arange(quarter, dtype=F32) / quarter)
    ar = jnp.arange(rows, dtype=F32)[:, None] * freqs
    ac = jnp.arange(GRID_W, dtype=F32)[:, None] * freqs
    return (jnp.concatenate([jnp.sin(ar), jnp.cos(ar)], axis=-1),
            jnp.concatenate([jnp.sin(ac), jnp.cos(ac)], axis=-1))


def kernel(x, c, ctx, c_ctx, w_ada, b_ada, g_ffn1, w_ffn1_in, w_ffn1_out, g_mix, w_in, b_gates,
           conv_qk_w, conv_qk_b, g_head, g_sgu, w_s, b_s, w_branch_a, w_branch_b, w_out,
           g_ffn2, w_ffn2_in, w_ffn2_out, g_final):
    bsz, seq, d = x.shape
    ctx_len = ctx.shape[1]
    layer = 0
    pos = _pos_tables(seq // GRID_W)

    cond = jnp.concatenate([c, c_ctx[None], jnp.zeros((HALO - bsz - 1, d), F32)], axis=0)
    mods = _mods_call(cond, w_ada[layer], b_ada[layer]).reshape(HALO, N_MOD, d)
    lat_row = lambda b: b
    ctx_row = lambda b: bsz

    w1_in = w_ffn1_in[layer].astype(BF16)
    w1_out = w_ffn1_out[layer].astype(BF16)
    w2_in = w_ffn2_in[layer].astype(BF16)
    w2_out = w_ffn2_out[layer].astype(BF16)
    wi = w_in[layer]
    wqk = wi[:, :2 * d].astype(BF16)
    wv = wi[:, 2 * d:3 * d].astype(BF16)
    wgt = wi[:, 3 * d:3 * d + N_GATES].T.astype(BF16)
    wrest = wi[:, 3 * d + N_GATES:].astype(BF16)
    bg = b_gates[layer]

    def bg_row(tm):
        return jnp.broadcast_to(bg[:, None], (N_GATES, tm))

    def dec_table(dec):
        return dec[:, :, 0].reshape(-1)

    h = _ffn_call(x, pos, mods, lat_row, g_ffn1[layer], w1_in, w1_out,
                  mod_base=0, tm=FFN_TM, name="ffn1")
    hc = _ffn_call(ctx, None, mods, ctx_row, g_ffn1[layer], w1_in, w1_out,
                   mod_base=0, tm=ctx_len, name="ffn1_ctx")

    _, k_c, kt_c, v_c, gr_c = _proj_call(
        hc, mods, ctx_row, g_mix[layer], wqk, wv, wgt, None, conv_qk_w[layer], conv_qk_b[layer],
        bg_row(ctx_len), tm=ctx_len, name="proj_ctx")
    q_l, k_l, kt_l, v_l, gr_l, o_l, u_l, vs_l, ga_l, gb_l = _proj_call(
        h, mods, lat_row, g_mix[layer], wqk, wv, wgt, wrest, conv_qk_w[layer], conv_qk_b[layer],
        bg_row(PROJ_TM), tm=PROJ_TM, name="proj")

    ws_cf, ws_cb, dec_c, m_c = _mgate_call(gr_c, None, chunk=MLSTM_L, emit=False, name="mgate_ctx")
    p_f, p_b, rep_f, rep_b, ws_f, ws_b, dec_l = _mgate_call(
        gr_l, m_c, chunk=MLSTM_L, emit=True, name="mgate")
    state = _mlstm_call(dec_table(dec_c), None, kt_c, k_c, v_c, (ws_cf,), (ws_cb,), None,
                        chunk=MLSTM_L, emit=False, name="mlstm_ctx")
    hf, hb = _mlstm_call(dec_table(dec_l), q_l, kt_l, k_l, v_l, (p_f, rep_f, ws_f), (p_b, rep_b, ws_b),
                         state, chunk=MLSTM_L, emit=True, name="mlstm")

    return _tail_call(hf, hb, o_l, u_l, vs_l, ga_l, gb_l, h, mods, g_head[layer], g_sgu[layer],
                      w_s[layer].astype(BF16), b_s[layer].T, w_branch_a[layer].astype(BF16),
                      w_branch_b[layer].astype(BF16), w_out[layer].astype(BF16),
                      g_ffn2[layer], w2_in, w2_out, g_final, tm=TAIL_TM)
```

```python
import functools
import math

import jax
import jax.numpy as jnp
from jax import lax
from jax.experimental import pallas as pl
from jax.experimental.pallas import tpu as pltpu

F32 = jnp.float32
BF16 = jnp.bfloat16

D_MODEL = 1024
GRID_W = 64
N_HEADS = 4
HEAD_DIM = 256
N_GROUPS = 4
GROUP_DIM = 256
SGU_CHUNK = 128
D_FF = 2816
N_MOD = 9
N_GATES = 16
POS_BASE = 10000.0
EPS = 1e-6

LANES = 128
HALO = 8
VMEM_LIMIT = 56 * 1024 * 1024
MASK_NEG = -1e30

FFN_TM = 512
PROJ_TM = 512
TAIL_TM = 256
MLSTM_L = 256
FF_CHUNKS = ((0, 1024), (1024, 2048), (2048, D_FF))


def _dot(a, b):
    return jnp.dot(a, b, preferred_element_type=F32)


def _sigmoid(x):
    return 1.0 / (1.0 + jnp.exp(-x))


def _gelu_tanh(x):
    c = math.sqrt(2.0 / math.pi)
    return 0.5 * x * (1.0 + jnp.tanh(c * (x + 0.044715 * (x * x * x))))


def _log_sigmoid(x):
    return jnp.minimum(x, 0.0) - jnp.log(1.0 + jnp.exp(-jnp.abs(x)))


def _rms(x, g):
    return x * lax.rsqrt(jnp.mean(x * x, axis=-1, keepdims=True) + EPS) * g


def _norm_mod(x, g, shift, scale):
    return _rms(x, g) * (1.0 + scale) + shift


def _const_spec(shape):
    zeros = (0,) * len(shape)
    return pl.BlockSpec(shape, lambda *_: zeros, pipeline_mode=pl.Buffered(1))


def _params(sem):
    return pltpu.CompilerParams(dimension_semantics=sem, vmem_limit_bytes=VMEM_LIMIT)


def _mods_kernel(c_ref, w_ref, b_ref, o_ref):
    c = c_ref[...]
    s = (c * _sigmoid(c)).astype(BF16)
    o_ref[...] = _dot(s, w_ref[...].astype(BF16)) + b_ref[...]


def _mods_call(cond, w_ada, b_ada):
    rows, d = cond.shape
    n = w_ada.shape[1]
    tn = 1024
    return pl.pallas_call(
        _mods_kernel,
        out_shape=jax.ShapeDtypeStruct((rows, n), F32),
        grid=(n // tn,),
        in_specs=[
            pl.BlockSpec((rows, d), lambda j: (0, 0)),
            pl.BlockSpec((d, tn), lambda j: (0, j)),
            pl.BlockSpec((1, tn), lambda j: (0, j)),
        ],
        out_specs=pl.BlockSpec((rows, tn), lambda j: (0, j)),
        compiler_params=_params(("arbitrary",)),
        name="mods",
    )(cond, w_ada, b_ada.reshape(1, n))


def _ffn_body(x, m, mod_base, g, win_ref, wout_ref):
    shift = m[mod_base:mod_base + 1]
    scale = m[mod_base + 1:mod_base + 2]
    gate = m[mod_base + 2:mod_base + 3]
    hb = _norm_mod(x, g, shift, scale).astype(BF16)
    y = None
    for c0, c1 in FF_CHUNKS:
        a = _dot(hb, win_ref[:, c0:c1])
        b = _dot(hb, win_ref[:, D_FF + c0:D_FF + c1])
        act = (a * _sigmoid(a) * b).astype(BF16)
        part = _dot(act, wout_ref[c0:c1, :])
        y = part if y is None else y + part
    return x + 0.5 * gate * y


def _ffn_kernel(*refs, tm, mod_base, add_pos, cast_cols):
    it = iter(refs)
    x_ref = next(it)
    prow_ref, pcol_ref = (next(it), next(it)) if add_pos else (None, None)
    mods_ref, g_ref, win_ref, wout_ref = (next(it) for _ in range(4))
    cast_in = [next(it) for _ in cast_cols]
    o_ref = next(it)
    cast_out = [[next(it) for _ in cols] for cols in cast_cols]

    x = x_ref[0]
    if add_pos:
        pcol = pcol_ref[...]
        half = pcol.shape[1]
        pos = [jnp.concatenate([jnp.broadcast_to(prow_ref[r:r + 1, :], (GRID_W, half)), pcol], axis=-1)
               for r in range(tm // GRID_W)]
        x = x + jnp.concatenate(pos, axis=0)
    o_ref[0] = _ffn_body(x, mods_ref[0], mod_base, g_ref[...], win_ref, wout_ref)

    for src, cols, outs in zip(cast_in, cast_cols, cast_out):
        for (c0, c1), dst in zip(cols, outs):
            dst[...] = src[:, c0:c1].astype(BF16)


def _ffn_call(x, pos, mods, mod_row, g, w_in, w_out, casts, *, mod_base, tm, name):
    bsz, s, d = x.shape
    add_pos = pos is not None
    nstep = (s // tm) * bsz
    args = [x]
    specs = [pl.BlockSpec((1, tm, d), lambda i, b: (b, i, 0))]
    if add_pos:
        prow, pcol = pos
        args += [prow, pcol]
        specs += [pl.BlockSpec((tm // GRID_W, prow.shape[1]), lambda i, b: (i, 0)),
                  _const_spec(pcol.shape)]
    args += [mods, g.reshape(1, d), w_in, w_out]
    specs += [
        pl.BlockSpec((1, N_MOD, d), lambda i, b: (mod_row(b), 0, 0)),
        _const_spec((1, d)),
        _const_spec(w_in.shape),
        _const_spec(w_out.shape),
    ]
    out_shape = [jax.ShapeDtypeStruct((bsz, s, d), F32)]
    out_specs = [pl.BlockSpec((1, tm, d), lambda i, b: (b, i, 0))]
    for w, rows, cols in casts:
        nblk = w.shape[0] // rows
        assert nblk * rows == w.shape[0] and nblk <= nstep
        blk = lambda i, b, nblk=nblk: (jnp.minimum(i * bsz + b, nblk - 1), 0)
        args.append(w)
        specs.append(pl.BlockSpec((rows, w.shape[1]), blk))
        for c0, c1 in cols:
            out_shape.append(jax.ShapeDtypeStruct((w.shape[0], c1 - c0), BF16))
            out_specs.append(pl.BlockSpec((rows, c1 - c0), blk))
    kern = functools.partial(_ffn_kernel, tm=tm, mod_base=mod_base, add_pos=add_pos,
                             cast_cols=tuple(c for _, _, c in casts))
    return pl.pallas_call(
        kern,
        out_shape=out_shape,
        grid=(s // tm, bsz),
        in_specs=specs,
        out_specs=out_specs,
        compiler_params=_params(("arbitrary", "arbitrary")),
        name=name,
    )(*args)


def _proj_kernel(*refs, tm, with_rest):
    it = iter(refs)
    h_ref, hp_ref, hx_ref, mods_ref, g_ref = (next(it) for _ in range(5))
    wqk_ref, wv_ref, wgt_ref = (next(it) for _ in range(3))
    wrest_ref = next(it) if with_rest else None
    cw_ref, cb_ref, bgr_ref = (next(it) for _ in range(3))
    q_ref, k_ref, kt_ref, v_ref, gr_ref = (next(it) for _ in range(5))
    rest_refs = [next(it) for _ in range(5)] if with_rest else []
    raw_ref = next(it)

    i = pl.program_id(0)
    nt = pl.num_programs(0)
    m = mods_ref[0]
    shift, scale = m[3:4], m[4:5]
    g = g_ref[...]
    hn = _norm_mod(h_ref[0], g, shift, scale)
    prev_ok = (i > 0).astype(F32)
    next_ok = (i < nt - 1).astype(F32)
    hn_prev = _norm_mod(hp_ref[0], g, shift, scale) * prev_ok
    hn_next = _norm_mod(hx_ref[0], g, shift, scale) * next_ok
    hb = hn.astype(BF16)
    ext = jnp.concatenate([hn_prev, hn, hn_next], axis=0).astype(BF16)

    raw_ref[...] = _dot(ext, wqk_ref[...])
    sub8 = lax.broadcasted_iota(jnp.int32, (HALO, HEAD_DIM), 0)

    def conv_chunk(c):
        cols = slice(c * HEAD_DIM, (c + 1) * HEAD_DIM)
        cw = cw_ref[:, cols]
        x0 = raw_ref[HALO:HALO + tm, cols]
        down = pltpu.roll(x0, 1, axis=0)
        first = jnp.where(sub8 == 0, raw_ref[HALO - 1:HALO, cols], down[:HALO])
        xm = jnp.concatenate([first, down[HALO:]], axis=0)
        up = pltpu.roll(x0, tm - 1, axis=0)
        last = jnp.where(sub8 == HALO - 1, raw_ref[HALO + tm:HALO + tm + 1, cols], up[tm - HALO:])
        xp = jnp.concatenate([up[:tm - HALO], last], axis=0)
        acc = xm * cw[0:1] + x0 * cw[1:2] + xp * cw[2:3] + cb_ref[:, cols]
        act = acc * _sigmoid(acc)
        if c < N_HEADS:
            q_ref[0, :, cols] = (act * HEAD_DIM ** -0.5).astype(BF16)
        else:
            kc = slice((c - N_HEADS) * HEAD_DIM, (c - N_HEADS + 1) * HEAD_DIM)
            k_ref[0, :, kc] = act.astype(BF16)
            kt_ref[0, kc, :] = act.T.astype(BF16)

    def rest_piece(j):
        rest_refs[j][0] = _dot(hb, wrest_ref[:, j * D_MODEL:(j + 1) * D_MODEL]).astype(BF16)

    grow = lax.dot_general(wgt_ref[...], hb, (((1,), (1,)), ((), ())),
                           preferred_element_type=F32) + bgr_ref[...]
    sub = lax.broadcasted_iota(jnp.int32, grow.shape, 0)
    gr_ref[0] = jnp.where(sub % 8 >= N_HEADS, _log_sigmoid(grow), grow)

    v_ref[0] = _dot(hb, wv_ref[...]).astype(BF16)
    for c in range(2 * N_HEADS):
        if with_rest and c < 5:
            rest_piece(c)
        conv_chunk(c)


def _proj_call(h, mods, mod_row, g, wqk, wv, wgt, wrest, conv_w, conv_b, bg_row, *, tm, name):
    bsz, s, d = h.shape
    with_rest = wrest is not None
    nhb = tm // HALO
    last = s // HALO - 1
    args = [h, h, h, mods, g.reshape(1, d), wqk, wv, wgt]
    specs = [
        pl.BlockSpec((1, tm, d), lambda i, b: (b, i, 0)),
        pl.BlockSpec((1, HALO, d), lambda i, b: (b, jnp.maximum(i * nhb - 1, 0), 0)),
        pl.BlockSpec((1, HALO, d), lambda i, b: (b, jnp.minimum((i + 1) * nhb, last), 0)),
        pl.BlockSpec((1, N_MOD, d), lambda i, b: (mod_row(b), 0, 0)),
        _const_spec((1, d)),
        _const_spec(wqk.shape), _const_spec(wv.shape), _const_spec(wgt.shape),
    ]
    if with_rest:
        args.append(wrest)
        specs.append(_const_spec(wrest.shape))
    args += [conv_w, conv_b.reshape(1, -1), bg_row]
    specs += [_const_spec(conv_w.shape), _const_spec((1, conv_b.shape[0])), _const_spec(bg_row.shape)]
    tok = lambda i, b: (b, i, 0)
    out_shape = [
        jax.ShapeDtypeStruct((bsz, s, d), BF16),
        jax.ShapeDtypeStruct((bsz, s, d), BF16),
        jax.ShapeDtypeStruct((bsz, d, s), BF16),
        jax.ShapeDtypeStruct((bsz, s, d), BF16),
        jax.ShapeDtypeStruct((bsz, N_GATES, s), F32),
    ]
    out_specs = [
        pl.BlockSpec((1, tm, d), tok), pl.BlockSpec((1, tm, d), tok),
        pl.BlockSpec((1, d, tm), lambda i, b: (b, 0, i)),
        pl.BlockSpec((1, tm, d), tok),
        pl.BlockSpec((1, N_GATES, tm), lambda i, b: (b, 0, i)),
    ]
    if with_rest:
        out_shape += [jax.ShapeDtypeStruct((bsz, s, d), BF16)] * 5
        out_specs += [pl.BlockSpec((1, tm, d), tok)] * 5
    kern = functools.partial(_proj_kernel, tm=tm, with_rest=with_rest)
    return pl.pallas_call(
        kern,
        out_shape=out_shape,
        grid=(s // tm, bsz),
        in_specs=specs,
        out_specs=out_specs,
        scratch_shapes=[pltpu.VMEM((tm + 2 * HALO, 2 * d), F32)],
        compiler_params=_params(("arbitrary", "arbitrary")),
        name=name,
    )(*args)


def _split3(x):
    hi = x.astype(BF16)
    r1 = x - hi.astype(F32)
    mid = r1.astype(BF16)
    lo = (r1 - mid.astype(F32)).astype(BF16)
    return hi, mid, lo


def _chain(d, b, hd, nb):
    return (d * nb + b) * N_HEADS + hd


def _mgate_kernel(*refs, chunk, nb, emit, has_init):
    it = iter(refs)
    gr_refs = (next(it), next(it))
    m0_ref = next(it) if has_init else None
    p_refs = (next(it), next(it)) if emit else None
    rep_refs = (next(it), next(it)) if emit else None
    ws_refs = (next(it), next(it))
    dec_ref = next(it)
    m_ref = next(it)

    @pl.when(pl.program_id(0) == 0)
    def _():
        m_ref[...] = m0_ref[...] if has_init else jnp.zeros_like(m_ref)

    row = lax.broadcasted_iota(jnp.int32, (chunk, chunk), 0)
    col = lax.broadcasted_iota(jnp.int32, (chunk, chunk), 1)
    lane = lax.broadcasted_iota(jnp.int32, (nb * N_HEADS, chunk), 1)
    ones = jnp.ones((chunk, chunk), BF16)
    sel_r = lax.broadcasted_iota(jnp.int32, (LANES, 2 * N_HEADS * LANES), 0)
    sel_j = lax.broadcasted_iota(jnp.int32, (LANES, 2 * N_HEADS * LANES), 1) // LANES
    spread = jnp.where(sel_r == sel_j + N_HEADS, 1.0, 0.0).astype(BF16)

    visible = (row >= col, row <= col)
    tri_ones = [jnp.concatenate([jnp.where(keep, 1.0, 0.0).astype(BF16), ones], axis=1)
                for keep in (row <= col, row >= col)]

    li, bq, g4 = [], [], []
    for d, gr_ref in enumerate(gr_refs):
        parts = []
        for b in range(nb):
            gr = gr_ref[b]
            sums = _dot(jnp.concatenate(_split3(gr), axis=0), tri_ones[d])
            sums = sums[:N_GATES] + sums[N_GATES:2 * N_GATES] + sums[2 * N_GATES:]
            b_all = sums[:, :chunk]
            g_all = sums[:, chunk:]
            f_rows = slice(8 * d + N_HEADS, 8 * d + 2 * N_HEADS)
            parts.append((gr[8 * d:8 * d + N_HEADS], b_all[f_rows], g_all[f_rows]))
        li.append(jnp.concatenate([p[0] for p in parts], axis=0))
        bq.append(jnp.concatenate([p[1] for p in parts], axis=0))
        g4.append(jnp.concatenate([p[2] for p in parts], axis=0))

    r, m_st = [], []
    for d in range(2):
        r.append(li[d] - bq[d])
        m_st.append(m_ref[d])
        m_new = g4[d] + jnp.maximum(m_st[d], jnp.max(r[d], axis=-1, keepdims=True))
        ws = jnp.exp(g4[d] + r[d] - m_new)
        for b in range(nb):
            ws_refs[d][b] = ws[b * N_HEADS:(b + 1) * N_HEADS]
        dec_ref[0, d * nb * N_HEADS:(d + 1) * nb * N_HEADS, :] = jnp.exp(g4[d] + m_st[d] - m_new)[:, :LANES]
        m_ref[d] = m_new
    if not emit:
        return

    run = list(r)
    k = 1
    while k < chunk:
        for d in range(2):
            if d == 0:
                shifted = jnp.where(lane >= k, pltpu.roll(run[d], k, axis=1), MASK_NEG)
            else:
                shifted = jnp.where(lane < chunk - k, pltpu.roll(run[d], chunk - k, axis=1), MASK_NEG)
            run[d] = jnp.maximum(run[d], shifted)
        k *= 2

    cols = {}
    for d in range(2):
        m4 = jnp.maximum(m_st[d], run[d])
        inter = jnp.exp(m_st[d] - m4)
        ei = jnp.exp(-bq[d] - m4)
        for b in range(nb):
            sl = slice(b * N_HEADS, (b + 1) * N_HEADS)
            rows = jnp.concatenate([m4[sl], inter[sl], ei[sl],
                                    jnp.zeros((LANES - 3 * N_HEADS, chunk), F32)], axis=0)
            cols[d, b] = rows.T
    for (d, b), c in cols.items():
        rep_refs[d][b] = _dot(c.astype(BF16), spread).astype(BF16)
    for (d, b), c in cols.items():
        for hd in range(N_HEADS):
            src_row = r[d][b * N_HEADS + hd:b * N_HEADS + hd + 1, :]
            p = jnp.where(visible[d], jnp.exp(src_row - c[:, hd:hd + 1]), 0.0)
            p_refs[d][b, :, hd * chunk:(hd + 1) * chunk] = p.astype(BF16)


def _mgate_call(gr, m0, *, chunk, emit, name):
    bsz, _, s = gr.shape
    nc = s // chunk
    ngroup = 2 * bsz
    has_init = m0 is not None
    fwd = lambda i: i
    bwd = lambda i: nc - 1 - i
    args = [gr, gr]
    specs = [pl.BlockSpec((bsz, N_GATES, chunk), lambda i: (0, 0, fwd(i))),
             pl.BlockSpec((bsz, N_GATES, chunk), lambda i: (0, 0, bwd(i)))]
    m_shape = (2, bsz * N_HEADS, chunk)
    if has_init:
        args.append(m0)
        specs.append(_const_spec(m_shape))
    out_shape, out_specs = [], []
    if emit:
        out_shape += [jax.ShapeDtypeStruct((bsz, s, N_HEADS * chunk), BF16)] * 2
        out_specs += [pl.BlockSpec((bsz, chunk, N_HEADS * chunk), lambda i: (0, fwd(i), 0)),
                      pl.BlockSpec((bsz, chunk, N_HEADS * chunk), lambda i: (0, bwd(i), 0))]
        out_shape += [jax.ShapeDtypeStruct((bsz, s, 2 * N_HEADS * LANES), BF16)] * 2
        out_specs += [pl.BlockSpec((bsz, chunk, 2 * N_HEADS * LANES), lambda i: (0, fwd(i), 0)),
                      pl.BlockSpec((bsz, chunk, 2 * N_HEADS * LANES), lambda i: (0, bwd(i), 0))]
    out_shape += [jax.ShapeDtypeStruct((bsz, N_HEADS, s), F32)] * 2
    out_specs += [pl.BlockSpec((bsz, N_HEADS, chunk), lambda i: (0, 0, fwd(i))),
                  pl.BlockSpec((bsz, N_HEADS, chunk), lambda i: (0, 0, bwd(i)))]
    out_shape.append(jax.ShapeDtypeStruct((nc, ngroup * N_HEADS, LANES), F32))
    out_specs.append(pl.BlockSpec((1, ngroup * N_HEADS, LANES), lambda i: (i, 0, 0)))
    if emit:
        scratch = [pltpu.VMEM(m_shape, F32)]
    else:
        out_shape.append(jax.ShapeDtypeStruct(m_shape, F32))
        out_specs.append(pl.BlockSpec(m_shape, lambda i: (0, 0, 0)))
        scratch = []
    kern = functools.partial(_mgate_kernel, chunk=chunk, nb=bsz, emit=emit, has_init=has_init)
    return pl.pallas_call(
        kern,
        out_shape=out_shape,
        grid=(nc,),
        in_specs=specs,
        out_specs=out_specs,
        scratch_shapes=scratch,
        compiler_params=_params(("arbitrary",)),
        name=name,
    )(*args)


def _fold_lanes(x):
    blocks = [x[:, c:c + LANES] for c in range(0, x.shape[1], LANES)]
    return functools.reduce(lambda a, b: a + b, blocks)


def _tile_lanes(x, width):
    return jnp.concatenate([x] * (width // LANES), axis=-1)


def _mlstm_kernel(*refs, chunk, nb, emit, has_init):
    it = iter(refs)
    dec_ref = next(it)
    n_in = 7 if emit else 4
    dirs = [tuple(next(it) for _ in range(n_in)) for _ in range(2)]
    init = (next(it), next(it)) if has_init else None
    h_refs = (next(it), next(it)) if emit else None
    c_ref, n_ref = next(it), next(it)

    step = pl.program_id(0)
    nchain = 2 * nb * N_HEADS

    @pl.when(step == 0)
    def _():
        if has_init:
            c_ref[...] = init[0][...]
            n_ref[...] = init[1][...]
        else:
            c_ref[...] = jnp.zeros_like(c_ref)
            n_ref[...] = jnp.zeros_like(n_ref)

    for d, dir_refs in enumerate(dirs):
        if emit:
            q_ref, kt_ref, k_ref, v_ref, p_ref, rep_ref, ws_ref = dir_refs
        else:
            kt_ref, k_ref, v_ref, ws_ref = dir_refs
        for b in range(nb):
            for hd in range(N_HEADS):
                j = _chain(d, b, hd, nb)
                lanes = slice(hd * HEAD_DIM, (hd + 1) * HEAD_DIM)
                decay = dec_ref[step * nchain + j]
                ws = ws_ref[b, hd:hd + 1, :].astype(BF16)
                kt = kt_ref[b, lanes, :]
                k = k_ref[b, :, lanes]
                v = v_ref[b, :, lanes]
                ct = c_ref[j]
                n_st = n_ref[j]
                if emit:
                    q = q_ref[b, :, lanes]
                    inter = rep_ref[b, :, hd * LANES:(hd + 1) * LANES].astype(F32)
                    ei = rep_ref[b, :, (N_HEADS + hd) * LANES:(N_HEADS + hd + 1) * LANES].astype(F32)
                    s_ts = _dot(q, kt) * p_ref[b, :, hd * chunk:(hd + 1) * chunk].astype(F32)
                    num_intra = _dot(s_ts.astype(BF16), v)
                    num_inter = _dot(q, ct.astype(BF16))
                    den = jnp.sum(_fold_lanes(s_ts) + inter * _fold_lanes(q.astype(F32) * n_st),
                                  axis=-1, keepdims=True)
                    rden = 1.0 / jnp.maximum(jnp.abs(den), ei)
                    h = (num_intra + _tile_lanes(inter, HEAD_DIM) * num_inter) * _tile_lanes(rden, HEAD_DIM)
                    h_refs[d][b, :, lanes] = h.astype(BF16)
                c_ref[j] = decay * ct + _dot(kt * ws, v)
                n_ref[j] = decay * n_st + _dot(ws, k)


def _mlstm_call(dec, q, kt, k, v, gate_f, gate_b, init, *, chunk, emit, name):
    bsz, d, s = kt.shape
    nc = s // chunk
    nchain = 2 * bsz * N_HEADS
    has_init = init is not None

    def dir_args(gate, idx):
        tok = lambda i: (0, idx(i), 0)
        feat = lambda i: (0, 0, idx(i))
        args, specs = [], []
        if emit:
            args.append(q)
            specs.append(pl.BlockSpec((bsz, chunk, d), tok))
        args += [kt, k, v]
        specs += [pl.BlockSpec((bsz, d, chunk), feat),
                  pl.BlockSpec((bsz, chunk, d), tok), pl.BlockSpec((bsz, chunk, d), tok)]
        if emit:
            p, rep, ws = gate
            args += [p, rep]
            specs += [pl.BlockSpec((bsz, chunk, N_HEADS * chunk), tok),
                      pl.BlockSpec((bsz, chunk, 2 * N_HEADS * LANES), tok)]
        else:
            (ws,) = gate
        args.append(ws)
        specs.append(pl.BlockSpec((bsz, N_HEADS, chunk), feat))
        return args, specs

    fwd = lambda i: i
    bwd = lambda i: nc - 1 - i
    af, sf = dir_args(gate_f, fwd)
    ab, sb = dir_args(gate_b, bwd)
    args = [dec] + af + ab
    specs = [pl.BlockSpec(memory_space=pltpu.SMEM)] + sf + sb
    state_shapes = [(nchain, HEAD_DIM, HEAD_DIM), (nchain, 1, HEAD_DIM)]
    if has_init:
        args += list(init)
        specs += [_const_spec(sh) for sh in state_shapes]
    kern = functools.partial(_mlstm_kernel, chunk=chunk, nb=bsz, emit=emit, has_init=has_init)
    if emit:
        out_shape = [jax.ShapeDtypeStruct((bsz, s, d), BF16)] * 2
        out_specs = [pl.BlockSpec((bsz, chunk, d), lambda i: (0, fwd(i), 0)),
                     pl.BlockSpec((bsz, chunk, d), lambda i: (0, bwd(i), 0))]
        scratch = [pltpu.VMEM(sh, F32) for sh in state_shapes]
    else:
        out_shape = [jax.ShapeDtypeStruct(sh, F32) for sh in state_shapes]
        out_specs = [pl.BlockSpec(sh, lambda i: (0, 0, 0)) for sh in state_shapes]
        scratch = []
    return pl.pallas_call(
        kern,
        out_shape=out_shape,
        grid=(nc,),
        in_specs=specs,
        out_specs=out_specs,
        scratch_shapes=scratch,
        compiler_params=_params(("arbitrary",)),
        name=name,
    )(*args)


def _tail_kernel(hf_ref, hb_ref, o_ref, u_ref, vs_ref, ga_ref, gb_ref, h_ref, mods_ref, mods_ffn_ref,
                 ghead_ref, gsgu_ref, ws_ref, bs_ref, wa_ref, wb_ref, wo_ref,
                 g2_ref, win_ref, wout_ref, gfin_ref, out_ref, yb_ref, h2_ref, *, tm):
    @pl.when(pl.program_id(0) == 0)
    def _():
        h2_ref[...] = jnp.zeros_like(h2_ref)

    h2_prev = h2_ref[...]
    out_ref[0] = _rms(_ffn_body(h2_prev, mods_ffn_ref[0], 6, g2_ref[...], win_ref, wout_ref), gfin_ref[...])

    m = mods_ref[0]
    hm = hf_ref[0].astype(F32) + hb_ref[0].astype(F32)
    parts = []
    for hd in range(N_HEADS):
        x = hm[:, hd * HEAD_DIM:(hd + 1) * HEAD_DIM]
        xc = x - jnp.mean(x, axis=-1, keepdims=True)
        parts.append(xc * lax.rsqrt(jnp.mean(xc * xc, axis=-1, keepdims=True) + EPS))
    ln = jnp.concatenate(parts, axis=-1) * ghead_ref[...]
    y_a = (_sigmoid(o_ref[0].astype(F32)) * ln).astype(BF16)

    vn = _rms(_gelu_tanh(vs_ref[0].astype(F32)), gsgu_ref[...]).astype(BF16)
    for r in range(tm // SGU_CHUNK):
        rows = slice(r * SGU_CHUNK, (r + 1) * SGU_CHUNK)
        for gidx in range(N_GROUPS):
            cols = slice(gidx * GROUP_DIM, (gidx + 1) * GROUP_DIM)
            mixed = _dot(ws_ref[gidx], vn[rows, cols]) + bs_ref[:, gidx:gidx + 1]
            gu = _gelu_tanh(u_ref[0, rows, cols].astype(F32))
            yb_ref[rows, cols] = (gu * mixed).astype(BF16)

    mixed_out = (_sigmoid(ga_ref[0].astype(F32)) * _dot(y_a, wa_ref[...])
                 + _sigmoid(gb_ref[0].astype(F32)) * _dot(yb_ref[...], wb_ref[...]))
    y = _dot(mixed_out.astype(BF16), wo_ref[...])
    h2_ref[...] = h_ref[0] + m[5:6] * y


def _tail_call(hf, hb, o, u, vs, ga, gb, h, mods, g_head, g_sgu, w_s, b_s_t, w_a, w_b, w_o,
               g2, w_in, w_out, g_final, *, tm):
    bsz, s, d = h.shape
    ntile = bsz * (s // tm)
    merge_tile = lambda t: jnp.minimum(t, ntile - 1)
    ffn_tile = lambda t: jnp.maximum(t - 1, 0)
    tok = pl.BlockSpec((1, tm, d), lambda t: (merge_tile(t) % bsz, merge_tile(t) // bsz, 0))
    vec = _const_spec((1, d))
    specs = [tok] * 8 + [
        pl.BlockSpec((1, N_MOD, d), lambda t: (merge_tile(t) % bsz, 0, 0)),
        pl.BlockSpec((1, N_MOD, d), lambda t: (ffn_tile(t) % bsz, 0, 0)),
        vec, vec, _const_spec(w_s.shape), _const_spec(b_s_t.shape),
        _const_spec(w_a.shape), _const_spec(w_b.shape), _const_spec(w_o.shape),
        vec, _const_spec(w_in.shape), _const_spec(w_out.shape), vec,
    ]
    return pl.pallas_call(
        functools.partial(_tail_kernel, tm=tm),
        out_shape=jax.ShapeDtypeStruct((bsz, s, d), F32),
        grid=(ntile + 1,),
        in_specs=specs,
        out_specs=pl.BlockSpec((1, tm, d), lambda t: (ffn_tile(t) % bsz, ffn_tile(t) // bsz, 0)),
        scratch_shapes=[pltpu.VMEM((tm, d), BF16), pltpu.VMEM((tm, d), F32)],
        compiler_params=_params(("arbitrary",)),
        name="tail",
    )(hf, hb, o, u, vs, ga, gb, h, mods, mods, g_head.reshape(1, d), g_sgu.reshape(1, d),
      w_s, b_s_t, w_a, w_b, w_o, g2.reshape(1, d), w_in, w_out, g_final.reshape(1, d))


def _pos_tables(rows):
    quarter = D_MODEL // 4
    freqs = jnp.exp(-math.log(POS_BASE) * jnp.arange(quarter, dtype=F32) / quarter)
    ar = jnp.arange(rows, dtype=F32)[:, None] * freqs
    ac = jnp.arange(GRID_W, dtype=F32)[:, None] * freqs
    return (jnp.concatenate([jnp.sin(ar), jnp.cos(ar)], axis=-1),
            jnp.concatenate([jnp.sin(ac), jnp.cos(ac)], axis=-1))


def kernel(x, c, ctx, c_ctx, w_ada, b_ada, g_ffn1, w_ffn1_in, w_ffn1_out, g_mix, w_in, b_gates,
           conv_qk_w, conv_qk_b, g_head, g_sgu, w_s, b_s, w_branch_a, w_branch_b, w_out,
           g_ffn2, w_ffn2_in, w_ffn2_out, g_final):
    bsz, seq, d = x.shape
    ctx_len = ctx.shape[1]
    layer = 0
    pos = _pos_tables(seq // GRID_W)

    cond = jnp.concatenate([c, c_ctx[None], jnp.zeros((HALO - bsz - 1, d), F32)], axis=0)
    mods = _mods_call(cond, w_ada[layer], b_ada[layer]).reshape(HALO, N_MOD, d)
    lat_row = lambda b: b
    ctx_row = lambda b: bsz

    w1_in = w_ffn1_in[layer].astype(BF16)
    w1_out = w_ffn1_out[layer].astype(BF16)
    wi = w_in[layer]
    wgt = wi[:, 3 * d:3 * d + N_GATES].T.astype(BF16)
    bg = b_gates[layer]
    nstep = (seq // FFN_TM) * bsz
    casts = [
        (wi, d // nstep, ((0, 2 * d), (2 * d, 3 * d), (3 * d + N_GATES, wi.shape[1]))),
        (w_ffn2_in[layer], d // nstep, ((0, 2 * D_FF),)),
        (w_ffn2_out[layer], 2 * D_FF // nstep, ((0, d),)),
        (w_branch_a[layer], d // nstep, ((0, d),)),
        (w_branch_b[layer], d // nstep, ((0, d),)),
        (w_out[layer], d // nstep, ((0, d),)),
    ]

    def bg_row(tm):
        return jnp.broadcast_to(bg[:, None], (N_GATES, tm))

    def dec_table(dec):
        return dec[:, :, 0].reshape(-1)

    h, wqk, wv, wrest, w2_in, w2_out, wa, wb, wo = _ffn_call(
        x, pos, mods, lat_row, g_ffn1[layer], w1_in, w1_out, casts, mod_base=0, tm=FFN_TM, name="ffn1")
    (hc,) = _ffn_call(ctx, None, mods, ctx_row, g_ffn1[layer], w1_in, w1_out, [],
                      mod_base=0, tm=ctx_len, name="ffn1_ctx")

    _, k_c, kt_c, v_c, gr_c = _proj_call(
        hc, mods, ctx_row, g_mix[layer], wqk, wv, wgt, None, conv_qk_w[layer], conv_qk_b[layer],
        bg_row(ctx_len), tm=ctx_len, name="proj_ctx")
    q_l, k_l, kt_l, v_l, gr_l, o_l, u_l, vs_l, ga_l, gb_l = _proj_call(
        h, mods, lat_row, g_mix[layer], wqk, wv, wgt, wrest, conv_qk_w[layer], conv_qk_b[layer],
        bg_row(PROJ_TM), tm=PROJ_TM, name="proj")

    ws_cf, ws_cb, dec_c, m_c = _mgate_call(gr_c, None, chunk=MLSTM_L, emit=False, name="mgate_ctx")
    p_f, p_b, rep_f, rep_b, ws_f, ws_b, dec_l = _mgate_call(
        gr_l, m_c, chunk=MLSTM_L, emit=True, name="mgate")
    state = _mlstm_call(dec_table(dec_c), None, kt_c, k_c, v_c, (ws_cf,), (ws_cb,), None,
                        chunk=MLSTM_L, emit=False, name="mlstm_ctx")
    hf, hb = _mlstm_call(dec_table(dec_l), q_l, kt_l, k_l, v_l, (p_f, rep_f, ws_f), (p_b, rep_b, ws_b),
                         state, chunk=MLSTM_L, emit=True, name="mlstm")

    return _tail_call(hf, hb, o_l, u_l, vs_l, ga_l, gb_l, h, mods, g_head[layer], g_sgu[layer],
                      w_s[layer].astype(BF16), b_s[layer].T, wa, wb, wo,
                      g_ffn2[layer], w2_in, w2_out, g_final, tm=TAIL_TM)
```

```python
import functools
import math

import jax
import jax.numpy as jnp
from jax import lax
from jax.experimental import pallas as pl
from jax.experimental.pallas import tpu as pltpu

F32 = jnp.float32
BF16 = jnp.bfloat16

D_MODEL = 1024
GRID_W = 64
N_HEADS = 4
HEAD_DIM = 256
N_GROUPS = 4
GROUP_DIM = 256
SGU_CHUNK = 128
D_FF = 2816
N_MOD = 9
N_GATES = 16
POS_BASE = 10000.0
EPS = 1e-6

LANES = 128
HALO = 8
VMEM_LIMIT = 56 * 1024 * 1024
MASK_NEG = -1e30

FFN_TM = 512
PROJ_TM = 512
TAIL_TM = 256
MLSTM_L = 256
FF_CHUNKS = ((0, 1024), (1024, 2048), (2048, D_FF))


def _dot(a, b):
    return jnp.dot(a, b, preferred_element_type=F32)


def _dot_t(a, b_t):
    return lax.dot_general(a, b_t, (((1,), (1,)), ((), ())), preferred_element_type=F32)


def _sigmoid(x):
    return 1.0 / (1.0 + jnp.exp(-x))


def _gelu_tanh(x):
    c = math.sqrt(2.0 / math.pi)
    return 0.5 * x * (1.0 + jnp.tanh(c * (x + 0.044715 * (x * x * x))))


def _log_sigmoid(x):
    return jnp.minimum(x, 0.0) - jnp.log(1.0 + jnp.exp(-jnp.abs(x)))


def _rms(x, g):
    return x * lax.rsqrt(jnp.mean(x * x, axis=-1, keepdims=True) + EPS) * g


def _norm_mod(x, g, shift, scale):
    return _rms(x, g) * (1.0 + scale) + shift


def _const_spec(shape):
    zeros = (0,) * len(shape)
    return pl.BlockSpec(shape, lambda *_: zeros, pipeline_mode=pl.Buffered(1))


def _params(sem):
    return pltpu.CompilerParams(dimension_semantics=sem, vmem_limit_bytes=VMEM_LIMIT)


def _mods_kernel(c_ref, w_ref, b_ref, o_ref):
    c = c_ref[...]
    s = (c * _sigmoid(c)).astype(BF16)
    o_ref[...] = _dot(s, w_ref[...].astype(BF16)) + b_ref[...]


def _mods_call(cond, w_ada, b_ada):
    rows, d = cond.shape
    n = w_ada.shape[1]
    tn = 1024
    return pl.pallas_call(
        _mods_kernel,
        out_shape=jax.ShapeDtypeStruct((rows, n), F32),
        grid=(n // tn,),
        in_specs=[
            pl.BlockSpec((rows, d), lambda j: (0, 0)),
            pl.BlockSpec((d, tn), lambda j: (0, j)),
            pl.BlockSpec((1, tn), lambda j: (0, j)),
        ],
        out_specs=pl.BlockSpec((rows, tn), lambda j: (0, j)),
        compiler_params=_params(("arbitrary",)),
        name="mods",
    )(cond, w_ada, b_ada.reshape(1, n))


def _ffn_body(x, m, mod_base, g, win_ref, wout_ref):
    shift = m[mod_base:mod_base + 1]
    scale = m[mod_base + 1:mod_base + 2]
    gate = m[mod_base + 2:mod_base + 3]
    hb = _norm_mod(x, g, shift, scale).astype(BF16)
    y = None
    for c0, c1 in FF_CHUNKS:
        a = _dot(hb, win_ref[:, c0:c1])
        b = _dot(hb, win_ref[:, D_FF + c0:D_FF + c1])
        act = (a * _sigmoid(a) * b).astype(BF16)
        part = _dot(act, wout_ref[c0:c1, :])
        y = part if y is None else y + part
    return x + 0.5 * gate * y


def _ffn_kernel(*refs, tm, mod_base, add_pos, n_cast):
    it = iter(refs)
    x_ref = next(it)
    prow_ref, pcol_ref = (next(it), next(it)) if add_pos else (None, None)
    mods_ref, g_ref, win_ref, wout_ref = (next(it) for _ in range(4))
    cast_in = [next(it) for _ in range(n_cast)]
    o_ref = next(it)
    cast_out = [next(it) for _ in range(n_cast)]

    x = x_ref[0]
    if add_pos:
        pcol = pcol_ref[...]
        half = pcol.shape[1]
        pos = [jnp.concatenate([jnp.broadcast_to(prow_ref[r:r + 1, :], (GRID_W, half)), pcol], axis=-1)
               for r in range(tm // GRID_W)]
        x = x + jnp.concatenate(pos, axis=0)
    o_ref[0] = _ffn_body(x, mods_ref[0], mod_base, g_ref[...], win_ref, wout_ref)

    for src, dst in zip(cast_in, cast_out):
        dst[...] = src[...].astype(BF16)


def _ffn_call(x, pos, mods, mod_row, g, w_in, w_out, casts, *, mod_base, tm, name):
    bsz, s, d = x.shape
    add_pos = pos is not None
    nstep = (s // tm) * bsz
    args = [x]
    specs = [pl.BlockSpec((1, tm, d), lambda i, b: (b, i, 0))]
    if add_pos:
        prow, pcol = pos
        args += [prow, pcol]
        specs += [pl.BlockSpec((tm // GRID_W, prow.shape[1]), lambda i, b: (i, 0)),
                  _const_spec(pcol.shape)]
    args += [mods, g.reshape(1, d), w_in, w_out]
    specs += [
        pl.BlockSpec((1, N_MOD, d), lambda i, b: (mod_row(b), 0, 0)),
        _const_spec((1, d)),
        _const_spec(w_in.shape),
        _const_spec(w_out.shape),
    ]
    out_shape = [jax.ShapeDtypeStruct((bsz, s, d), F32)]
    out_specs = [pl.BlockSpec((1, tm, d), lambda i, b: (b, i, 0))]
    for w, first, count, rows in casts:
        nblk = count // rows
        assert nblk * rows == count and nblk <= nstep
        blk = lambda i, b, nblk=nblk: jnp.minimum(i * bsz + b, nblk - 1)
        args.append(w)
        specs.append(pl.BlockSpec((pl.Element(rows), pl.Element(w.shape[1])),
                                  lambda i, b, blk=blk, first=first, rows=rows:
                                  (pl.multiple_of(first + blk(i, b) * rows, HALO), 0)))
        out_shape.append(jax.ShapeDtypeStruct((count, w.shape[1]), BF16))
        out_specs.append(pl.BlockSpec((rows, w.shape[1]), lambda i, b, blk=blk: (blk(i, b), 0)))
    kern = functools.partial(_ffn_kernel, tm=tm, mod_base=mod_base, add_pos=add_pos, n_cast=len(casts))
    return pl.pallas_call(
        kern,
        out_shape=out_shape,
        grid=(s // tm, bsz),
        in_specs=specs,
        out_specs=out_specs,
        compiler_params=_params(("arbitrary", "arbitrary")),
        name=name,
    )(*args)


def _proj_kernel(*refs, tm, with_rest):
    it = iter(refs)
    h_ref, hp_ref, hx_ref, mods_ref, g_ref = (next(it) for _ in range(5))
    wqk_ref, wv_ref, wgt_ref = (next(it) for _ in range(3))
    wrest_ref = next(it) if with_rest else None
    cw_ref, cb_ref, bgr_ref = (next(it) for _ in range(3))
    q_ref, k_ref, kt_ref, v_ref, gr_ref = (next(it) for _ in range(5))
    rest_refs = [next(it) for _ in range(5)] if with_rest else []
    raw_ref = next(it)

    i = pl.program_id(0)
    nt = pl.num_programs(0)
    m = mods_ref[0]
    shift, scale = m[3:4], m[4:5]
    g = g_ref[...]
    hn = _norm_mod(h_ref[0], g, shift, scale)
    prev_ok = (i > 0).astype(F32)
    next_ok = (i < nt - 1).astype(F32)
    hn_prev = _norm_mod(hp_ref[0], g, shift, scale) * prev_ok
    hn_next = _norm_mod(hx_ref[0], g, shift, scale) * next_ok
    hb = hn.astype(BF16)
    ext = jnp.concatenate([hn_prev, hn, hn_next], axis=0).astype(BF16)

    raw_ref[...] = _dot_t(ext, wqk_ref[...])
    sub8 = lax.broadcasted_iota(jnp.int32, (HALO, HEAD_DIM), 0)

    def conv_chunk(c):
        cols = slice(c * HEAD_DIM, (c + 1) * HEAD_DIM)
        cw = cw_ref[:, cols]
        x0 = raw_ref[HALO:HALO + tm, cols]
        down = pltpu.roll(x0, 1, axis=0)
        first = jnp.where(sub8 == 0, raw_ref[HALO - 1:HALO, cols], down[:HALO])
        xm = jnp.concatenate([first, down[HALO:]], axis=0)
        up = pltpu.roll(x0, tm - 1, axis=0)
        last = jnp.where(sub8 == HALO - 1, raw_ref[HALO + tm:HALO + tm + 1, cols], up[tm - HALO:])
        xp = jnp.concatenate([up[:tm - HALO], last], axis=0)
        acc = xm * cw[0:1] + x0 * cw[1:2] + xp * cw[2:3] + cb_ref[:, cols]
        act = acc * _sigmoid(acc)
        if c < N_HEADS:
            q_ref[0, :, cols] = (act * HEAD_DIM ** -0.5).astype(BF16)
        else:
            kc = slice((c - N_HEADS) * HEAD_DIM, (c - N_HEADS + 1) * HEAD_DIM)
            k_ref[0, :, kc] = act.astype(BF16)
            kt_ref[0, kc, :] = act.T.astype(BF16)

    def rest_piece(j):
        rest_refs[j][0] = _dot_t(hb, wrest_ref[j * D_MODEL:(j + 1) * D_MODEL, :]).astype(BF16)

    grow = _dot_t(wgt_ref[...], hb) + bgr_ref[...]
    sub = lax.broadcasted_iota(jnp.int32, grow.shape, 0)
    gr_ref[0] = jnp.where(sub % 8 >= N_HEADS, _log_sigmoid(grow), grow)

    v_ref[0] = _dot_t(hb, wv_ref[...]).astype(BF16)
    for c in range(2 * N_HEADS):
        if with_rest and c < 5:
            rest_piece(c)
        conv_chunk(c)


def _proj_call(h, mods, mod_row, g, wqk, wv, wgt, wrest, conv_w, conv_b, bg_row, *, tm, name):
    bsz, s, d = h.shape
    with_rest = wrest is not None
    nhb = tm // HALO
    last = s // HALO - 1
    args = [h, h, h, mods, g.reshape(1, d), wqk, wv, wgt]
    specs = [
        pl.BlockSpec((1, tm, d), lambda i, b: (b, i, 0)),
        pl.BlockSpec((1, HALO, d), lambda i, b: (b, jnp.maximum(i * nhb - 1, 0), 0)),
        pl.BlockSpec((1, HALO, d), lambda i, b: (b, jnp.minimum((i + 1) * nhb, last), 0)),
        pl.BlockSpec((1, N_MOD, d), lambda i, b: (mod_row(b), 0, 0)),
        _const_spec((1, d)),
        _const_spec(wqk.shape), _const_spec(wv.shape), _const_spec(wgt.shape),
    ]
    if with_rest:
        args.append(wrest)
        specs.append(_const_spec(wrest.shape))
    args += [conv_w, conv_b.reshape(1, -1), bg_row]
    specs += [_const_spec(conv_w.shape), _const_spec((1, conv_b.shape[0])), _const_spec(bg_row.shape)]
    tok = lambda i, b: (b, i, 0)
    out_shape = [
        jax.ShapeDtypeStruct((bsz, s, d), BF16),
        jax.ShapeDtypeStruct((bsz, s, d), BF16),
        jax.ShapeDtypeStruct((bsz, d, s), BF16),
        jax.ShapeDtypeStruct((bsz, s, d), BF16),
        jax.ShapeDtypeStruct((bsz, N_GATES, s), F32),
    ]
    out_specs = [
        pl.BlockSpec((1, tm, d), tok), pl.BlockSpec((1, tm, d), tok),
        pl.BlockSpec((1, d, tm), lambda i, b: (b, 0, i)),
        pl.BlockSpec((1, tm, d), tok),
        pl.BlockSpec((1, N_GATES, tm), lambda i, b: (b, 0, i)),
    ]
    if with_rest:
        out_shape += [jax.ShapeDtypeStruct((bsz, s, d), BF16)] * 5
        out_specs += [pl.BlockSpec((1, tm, d), tok)] * 5
    kern = functools.partial(_proj_kernel, tm=tm, with_rest=with_rest)
    return pl.pallas_call(
        kern,
        out_shape=out_shape,
        grid=(s // tm, bsz),
        in_specs=specs,
        out_specs=out_specs,
        scratch_shapes=[pltpu.VMEM((tm + 2 * HALO, 2 * d), F32)],
        compiler_params=_params(("arbitrary", "arbitrary")),
        name=name,
    )(*args)


def _split3(x):
    hi = x.astype(BF16)
    r1 = x - hi.astype(F32)
    mid = r1.astype(BF16)
    lo = (r1 - mid.astype(F32)).astype(BF16)
    return hi, mid, lo


def _chain(d, b, hd, nb):
    return (d * nb + b) * N_HEADS + hd


def _mgate_kernel(*refs, chunk, nb, emit, has_init):
    it = iter(refs)
    gr_refs = (next(it), next(it))
    m0_ref = next(it) if has_init else None
    p_refs = (next(it), next(it)) if emit else None
    rep_refs = (next(it), next(it)) if emit else None
    ws_refs = (next(it), next(it))
    dec_ref = next(it)
    m_ref = next(it)

    @pl.when(pl.program_id(0) == 0)
    def _():
        m_ref[...] = m0_ref[...] if has_init else jnp.zeros_like(m_ref)

    row = lax.broadcasted_iota(jnp.int32, (chunk, chunk), 0)
    col = lax.broadcasted_iota(jnp.int32, (chunk, chunk), 1)
    lane = lax.broadcasted_iota(jnp.int32, (nb * N_HEADS, chunk), 1)
    ones = jnp.ones((chunk, chunk), BF16)
    sel_r = lax.broadcasted_iota(jnp.int32, (LANES, 2 * N_HEADS * LANES), 0)
    sel_j = lax.broadcasted_iota(jnp.int32, (LANES, 2 * N_HEADS * LANES), 1) // LANES
    spread = jnp.where(sel_r == sel_j + N_HEADS, 1.0, 0.0).astype(BF16)

    visible = (row >= col, row <= col)
    tri_ones = [jnp.concatenate([jnp.where(keep, 1.0, 0.0).astype(BF16), ones], axis=1)
                for keep in (row <= col, row >= col)]

    li, bq, g4 = [], [], []
    for d, gr_ref in enumerate(gr_refs):
        parts = []
        for b in range(nb):
            gr = gr_ref[b]
            sums = _dot(jnp.concatenate(_split3(gr), axis=0), tri_ones[d])
            sums = sums[:N_GATES] + sums[N_GATES:2 * N_GATES] + sums[2 * N_GATES:]
            b_all = sums[:, :chunk]
            g_all = sums[:, chunk:]
            f_rows = slice(8 * d + N_HEADS, 8 * d + 2 * N_HEADS)
            parts.append((gr[8 * d:8 * d + N_HEADS], b_all[f_rows], g_all[f_rows]))
        li.append(jnp.concatenate([p[0] for p in parts], axis=0))
        bq.append(jnp.concatenate([p[1] for p in parts], axis=0))
        g4.append(jnp.concatenate([p[2] for p in parts], axis=0))

    r, m_st = [], []
    for d in range(2):
        r.append(li[d] - bq[d])
        m_st.append(m_ref[d])
        m_new = g4[d] + jnp.maximum(m_st[d], jnp.max(r[d], axis=-1, keepdims=True))
        ws = jnp.exp(g4[d] + r[d] - m_new)
        for b in range(nb):
            ws_refs[d][b] = ws[b * N_HEADS:(b + 1) * N_HEADS]
        dec_ref[0, d * nb * N_HEADS:(d + 1) * nb * N_HEADS, :] = jnp.exp(g4[d] + m_st[d] - m_new)[:, :LANES]
        m_ref[d] = m_new
    if not emit:
        return

    run = list(r)
    k = 1
    while k < chunk:
        for d in range(2):
            if d == 0:
                shifted = jnp.where(lane >= k, pltpu.roll(run[d], k, axis=1), MASK_NEG)
            else:
                shifted = jnp.where(lane < chunk - k, pltpu.roll(run[d], chunk - k, axis=1), MASK_NEG)
            run[d] = jnp.maximum(run[d], shifted)
        k *= 2

    cols = {}
    for d in range(2):
        m4 = jnp.maximum(m_st[d], run[d])
        inter = jnp.exp(m_st[d] - m4)
        ei = jnp.exp(-bq[d] - m4)
        for b in range(nb):
            sl = slice(b * N_HEADS, (b + 1) * N_HEADS)
            rows = jnp.concatenate([m4[sl], inter[sl], ei[sl],
                                    jnp.zeros((LANES - 3 * N_HEADS, chunk), F32)], axis=0)
            cols[d, b] = rows.T
    for (d, b), c in cols.items():
        rep_refs[d][b] = _dot(c.astype(BF16), spread).astype(BF16)
    for (d, b), c in cols.items():
        for hd in range(N_HEADS):
            src_row = r[d][b * N_HEADS + hd:b * N_HEADS + hd + 1, :]
            p = jnp.where(visible[d], jnp.exp(src_row - c[:, hd:hd + 1]), 0.0)
            p_refs[d][b, :, hd * chunk:(hd + 1) * chunk] = p.astype(BF16)


def _mgate_call(gr, m0, *, chunk, emit, name):
    bsz, _, s = gr.shape
    nc = s // chunk
    ngroup = 2 * bsz
    has_init = m0 is not None
    fwd = lambda i: i
    bwd = lambda i: nc - 1 - i
    args = [gr, gr]
    specs = [pl.BlockSpec((bsz, N_GATES, chunk), lambda i: (0, 0, fwd(i))),
             pl.BlockSpec((bsz, N_GATES, chunk), lambda i: (0, 0, bwd(i)))]
    m_shape = (2, bsz * N_HEADS, chunk)
    if has_init:
        args.append(m0)
        specs.append(_const_spec(m_shape))
    out_shape, out_specs = [], []
    if emit:
        out_shape += [jax.ShapeDtypeStruct((bsz, s, N_HEADS * chunk), BF16)] * 2
        out_specs += [pl.BlockSpec((bsz, chunk, N_HEADS * chunk), lambda i: (0, fwd(i), 0)),
                      pl.BlockSpec((bsz, chunk, N_HEADS * chunk), lambda i: (0, bwd(i), 0))]
        out_shape += [jax.ShapeDtypeStruct((bsz, s, 2 * N_HEADS * LANES), BF16)] * 2
        out_specs += [pl.BlockSpec((bsz, chunk, 2 * N_HEADS * LANES), lambda i: (0, fwd(i), 0)),
                      pl.BlockSpec((bsz, chunk, 2 * N_HEADS * LANES), lambda i: (0, bwd(i), 0))]
    out_shape += [jax.ShapeDtypeStruct((bsz, N_HEADS, s), F32)] * 2
    out_specs += [pl.BlockSpec((bsz, N_HEADS, chunk), lambda i: (0, 0, fwd(i))),
                  pl.BlockSpec((bsz, N_HEADS, chunk), lambda i: (0, 0, bwd(i)))]
    out_shape.append(jax.ShapeDtypeStruct((nc, ngroup * N_HEADS, LANES), F32))
    out_specs.append(pl.BlockSpec((1, ngroup * N_HEADS, LANES), lambda i: (i, 0, 0)))
    if emit:
        scratch = [pltpu.VMEM(m_shape, F32)]
    else:
        out_shape.append(jax.ShapeDtypeStruct(m_shape, F32))
        out_specs.append(pl.BlockSpec(m_shape, lambda i: (0, 0, 0)))
        scratch = []
    kern = functools.partial(_mgate_kernel, chunk=chunk, nb=bsz, emit=emit, has_init=has_init)
    return pl.pallas_call(
        kern,
        out_shape=out_shape,
        grid=(nc,),
        in_specs=specs,
        out_specs=out_specs,
        scratch_shapes=scratch,
        compiler_params=_params(("arbitrary",)),
        name=name,
    )(*args)


def _fold_lanes(x):
    blocks = [x[:, c:c + LANES] for c in range(0, x.shape[1], LANES)]
    return functools.reduce(lambda a, b: a + b, blocks)


def _tile_lanes(x, width):
    return jnp.concatenate([x] * (width // LANES), axis=-1)


def _mlstm_kernel(*refs, chunk, nb, emit, has_init):
    it = iter(refs)
    dec_ref = next(it)
    n_in = 7 if emit else 4
    dirs = [tuple(next(it) for _ in range(n_in)) for _ in range(2)]
    init = (next(it), next(it)) if has_init else None
    h_refs = (next(it), next(it)) if emit else None
    c_ref, n_ref = next(it), next(it)

    step = pl.program_id(0)
    nchain = 2 * nb * N_HEADS

    @pl.when(step == 0)
    def _():
        if has_init:
            c_ref[...] = init[0][...]
            n_ref[...] = init[1][...]
        else:
            c_ref[...] = jnp.zeros_like(c_ref)
            n_ref[...] = jnp.zeros_like(n_ref)

    for d, dir_refs in enumerate(dirs):
        if emit:
            q_ref, kt_ref, k_ref, v_ref, p_ref, rep_ref, ws_ref = dir_refs
        else:
            kt_ref, k_ref, v_ref, ws_ref = dir_refs
        for b in range(nb):
            for hd in range(N_HEADS):
                j = _chain(d, b, hd, nb)
                lanes = slice(hd * HEAD_DIM, (hd + 1) * HEAD_DIM)
                decay = dec_ref[step * nchain + j]
                ws = ws_ref[b, hd:hd + 1, :].astype(BF16)
                kt = kt_ref[b, lanes, :]
                k = k_ref[b, :, lanes]
                v = v_ref[b, :, lanes]
                ct = c_ref[j]
                n_st = n_ref[j]
                if emit:
                    q = q_ref[b, :, lanes]
                    inter = rep_ref[b, :, hd * LANES:(hd + 1) * LANES].astype(F32)
                    ei = rep_ref[b, :, (N_HEADS + hd) * LANES:(N_HEADS + hd + 1) * LANES].astype(F32)
                    s_ts = _dot(q, kt) * p_ref[b, :, hd * chunk:(hd + 1) * chunk].astype(F32)
                    num_intra = _dot(s_ts.astype(BF16), v)
                    num_inter = _dot(q, ct.astype(BF16))
                    den = jnp.sum(_fold_lanes(s_ts) + inter * _fold_lanes(q.astype(F32) * n_st),
                                  axis=-1, keepdims=True)
                    rden = 1.0 / jnp.maximum(jnp.abs(den), ei)
                    h = (num_intra + _tile_lanes(inter, HEAD_DIM) * num_inter) * _tile_lanes(rden, HEAD_DIM)
                    h_refs[d][b, :, lanes] = h.astype(BF16)
                c_ref[j] = decay * ct + _dot(kt * ws, v)
                n_ref[j] = decay * n_st + _dot(ws, k)


def _mlstm_call(dec, q, kt, k, v, gate_f, gate_b, init, *, chunk, emit, name):
    bsz, d, s = kt.shape
    nc = s // chunk
    nchain = 2 * bsz * N_HEADS
    has_init = init is not None

    def dir_args(gate, idx):
        tok = lambda i: (0, idx(i), 0)
        feat = lambda i: (0, 0, idx(i))
        args, specs = [], []
        if emit:
            args.append(q)
            specs.append(pl.BlockSpec((bsz, chunk, d), tok))
        args += [kt, k, v]
        specs += [pl.BlockSpec((bsz, d, chunk), feat),
                  pl.BlockSpec((bsz, chunk, d), tok), pl.BlockSpec((bsz, chunk, d), tok)]
        if emit:
            p, rep, ws = gate
            args += [p, rep]
            specs += [pl.BlockSpec((bsz, chunk, N_HEADS * chunk), tok),
                      pl.BlockSpec((bsz, chunk, 2 * N_HEADS * LANES), tok)]
        else:
            (ws,) = gate
        args.append(ws)
        specs.append(pl.BlockSpec((bsz, N_HEADS, chunk), feat))
        return args, specs

    fwd = lambda i: i
    bwd = lambda i: nc - 1 - i
    af, sf = dir_args(gate_f, fwd)
    ab, sb = dir_args(gate_b, bwd)
    args = [dec] + af + ab
    specs = [pl.BlockSpec(memory_space=pltpu.SMEM)] + sf + sb
    state_shapes = [(nchain, HEAD_DIM, HEAD_DIM), (nchain, 1, HEAD_DIM)]
    if has_init:
        args += list(init)
        specs += [_const_spec(sh) for sh in state_shapes]
    kern = functools.partial(_mlstm_kernel, chunk=chunk, nb=bsz, emit=emit, has_init=has_init)
    if emit:
        out_shape = [jax.ShapeDtypeStruct((bsz, s, d), BF16)] * 2
        out_specs = [pl.BlockSpec((bsz, chunk, d), lambda i: (0, fwd(i), 0)),
                     pl.BlockSpec((bsz, chunk, d), lambda i: (0, bwd(i), 0))]
        scratch = [pltpu.VMEM(sh, F32) for sh in state_shapes]
    else:
        out_shape = [jax.ShapeDtypeStruct(sh, F32) for sh in state_shapes]
        out_specs = [pl.BlockSpec(sh, lambda i: (0, 0, 0)) for sh in state_shapes]
        scratch = []
    return pl.pallas_call(
        kern,
        out_shape=out_shape,
        grid=(nc,),
        in_specs=specs,
        out_specs=out_specs,
        scratch_shapes=scratch,
        compiler_params=_params(("arbitrary",)),
        name=name,
    )(*args)


def _tail_kernel(hf_ref, hb_ref, o_ref, u_ref, vs_ref, ga_ref, gb_ref, h_ref, mods_ref, mods_ffn_ref,
                 ghead_ref, gsgu_ref, ws_ref, bs_ref, wa_ref, wb_ref, wo_ref,
                 g2_ref, win_ref, wout_ref, gfin_ref, out_ref, yb_ref, h2_ref, *, tm):
    @pl.when(pl.program_id(0) == 0)
    def _():
        h2_ref[...] = jnp.zeros_like(h2_ref)

    h2_prev = h2_ref[...]
    out_ref[0] = _rms(_ffn_body(h2_prev, mods_ffn_ref[0], 6, g2_ref[...], win_ref, wout_ref), gfin_ref[...])

    m = mods_ref[0]
    hm = hf_ref[0].astype(F32) + hb_ref[0].astype(F32)
    parts = []
    for hd in range(N_HEADS):
        x = hm[:, hd * HEAD_DIM:(hd + 1) * HEAD_DIM]
        xc = x - jnp.mean(x, axis=-1, keepdims=True)
        parts.append(xc * lax.rsqrt(jnp.mean(xc * xc, axis=-1, keepdims=True) + EPS))
    ln = jnp.concatenate(parts, axis=-1) * ghead_ref[...]
    y_a = (_sigmoid(o_ref[0].astype(F32)) * ln).astype(BF16)

    vn = _rms(_gelu_tanh(vs_ref[0].astype(F32)), gsgu_ref[...]).astype(BF16)
    for r in range(tm // SGU_CHUNK):
        rows = slice(r * SGU_CHUNK, (r + 1) * SGU_CHUNK)
        for gidx in range(N_GROUPS):
            cols = slice(gidx * GROUP_DIM, (gidx + 1) * GROUP_DIM)
            mixed = _dot(ws_ref[gidx], vn[rows, cols]) + bs_ref[:, gidx:gidx + 1]
            gu = _gelu_tanh(u_ref[0, rows, cols].astype(F32))
            yb_ref[rows, cols] = (gu * mixed).astype(BF16)

    mixed_out = (_sigmoid(ga_ref[0].astype(F32)) * _dot(y_a, wa_ref[...])
                 + _sigmoid(gb_ref[0].astype(F32)) * _dot(yb_ref[...], wb_ref[...]))
    y = _dot(mixed_out.astype(BF16), wo_ref[...])
    h2_ref[...] = h_ref[0] + m[5:6] * y


def _tail_call(hf, hb, o, u, vs, ga, gb, h, mods, g_head, g_sgu, w_s, b_s_t, w_a, w_b, w_o,
               g2, w_in, w_out, g_final, *, tm):
    bsz, s, d = h.shape
    ntile = bsz * (s // tm)
    merge_tile = lambda t: jnp.minimum(t, ntile - 1)
    ffn_tile = lambda t: jnp.maximum(t - 1, 0)
    tok = pl.BlockSpec((1, tm, d), lambda t: (merge_tile(t) % bsz, merge_tile(t) // bsz, 0))
    vec = _const_spec((1, d))
    specs = [tok] * 8 + [
        pl.BlockSpec((1, N_MOD, d), lambda t: (merge_tile(t) % bsz, 0, 0)),
        pl.BlockSpec((1, N_MOD, d), lambda t: (ffn_tile(t) % bsz, 0, 0)),
        vec, vec, _const_spec(w_s.shape), _const_spec(b_s_t.shape),
        _const_spec(w_a.shape), _const_spec(w_b.shape), _const_spec(w_o.shape),
        vec, _const_spec(w_in.shape), _const_spec(w_out.shape), vec,
    ]
    return pl.pallas_call(
        functools.partial(_tail_kernel, tm=tm),
        out_shape=jax.ShapeDtypeStruct((bsz, s, d), F32),
        grid=(ntile + 1,),
        in_specs=specs,
        out_specs=pl.BlockSpec((1, tm, d), lambda t: (ffn_tile(t) % bsz, ffn_tile(t) // bsz, 0)),
        scratch_shapes=[pltpu.VMEM((tm, d), BF16), pltpu.VMEM((tm, d), F32)],
        compiler_params=_params(("arbitrary",)),
        name="tail",
    )(hf, hb, o, u, vs, ga, gb, h, mods, mods, g_head.reshape(1, d), g_sgu.reshape(1, d),
      w_s, b_s_t, w_a, w_b, w_o, g2.reshape(1, d), w_in, w_out, g_final.reshape(1, d))


def _pos_tables(rows):
    quarter = D_MODEL // 4
    freqs = jnp.exp(-math.log(POS_BASE) * jnp.arange(quarter, dtype=F32) / quarter)
    ar = jnp.arange(rows, dtype=F32)[:, None] * freqs
    ac = jnp.arange(GRID_W, dtype=F32)[:, None] * freqs
    return (jnp.concatenate([jnp.sin(ar), jnp.cos(ar)], axis=-1),
            jnp.concatenate([jnp.sin(ac), jnp.cos(ac)], axis=-1))


def kernel(x, c, ctx, c_ctx, w_ada, b_ada, g_ffn1, w_ffn1_in, w_ffn1_out, g_mix, w_in, b_gates,
           conv_qk_w, conv_qk_b, g_head, g_sgu, w_s, b_s, w_branch_a, w_branch_b, w_out,
           g_ffn2, w_ffn2_in, w_ffn2_out, g_final):
    bsz, seq, d = x.shape
    ctx_len = ctx.shape[1]
    layer = 0
    pos = _pos_tables(seq // GRID_W)

    cond = jnp.concatenate([c, c_ctx[None], jnp.zeros((HALO - bsz - 1, d), F32)], axis=0)
    mods = _mods_call(cond, w_ada[layer], b_ada[layer]).reshape(HALO, N_MOD, d)
    lat_row = lambda b: b
    ctx_row = lambda b: bsz

    w1_in = w_ffn1_in[layer].astype(BF16)
    w1_out = w_ffn1_out[layer].astype(BF16)
    wi_t = jnp.swapaxes(w_in[layer], 0, 1)
    wgt = wi_t[3 * d:3 * d + N_GATES].astype(BF16)
    bg = b_gates[layer]
    nstep = (seq // FFN_TM) * bsz
    n_rest = wi_t.shape[0] - 3 * d - N_GATES
    casts = [
        (wi_t, 0, 2 * d, 2 * d // nstep),
        (wi_t, 2 * d, d, d // nstep),
        (wi_t, 3 * d + N_GATES, n_rest, n_rest // nstep),
        (w_ffn2_in[layer], 0, d, d // nstep),
        (w_ffn2_out[layer], 0, D_FF, 2 * D_FF // nstep),
        (w_branch_a[layer], 0, d, d // nstep),
        (w_branch_b[layer], 0, d, d // nstep),
        (w_out[layer], 0, d, d // nstep),
    ]

    def bg_row(tm):
        return jnp.broadcast_to(bg[:, None], (N_GATES, tm))

    def dec_table(dec):
        return dec[:, :, 0].reshape(-1)

    h, wqk, wv, wrest, w2_in, w2_out, wa, wb, wo = _ffn_call(
        x, pos, mods, lat_row, g_ffn1[layer], w1_in, w1_out, casts, mod_base=0, tm=FFN_TM, name="ffn1")
    (hc,) = _ffn_call(ctx, None, mods, ctx_row, g_ffn1[layer], w1_in, w1_out, [],
                      mod_base=0, tm=ctx_len, name="ffn1_ctx")

    _, k_c, kt_c, v_c, gr_c = _proj_call(
        hc, mods, ctx_row, g_mix[layer], wqk, wv, wgt, None, conv_qk_w[layer], conv_qk_b[layer],
        bg_row(ctx_len), tm=ctx_len, name="proj_ctx")
    q_l, k_l, kt_l, v_l, gr_l, o_l, u_l, vs_l, ga_l, gb_l = _proj_call(
        h, mods, lat_row, g_mix[layer], wqk, wv, wgt, wrest, conv_qk_w[layer], conv_qk_b[layer],
        bg_row(PROJ_TM), tm=PROJ_TM, name="proj")

    ws_cf, ws_cb, dec_c, m_c = _mgate_call(gr_c, None, chunk=MLSTM_L, emit=False, name="mgate_ctx")
    p_f, p_b, rep_f, rep_b, ws_f, ws_b, dec_l = _mgate_call(
        gr_l, m_c, chunk=MLSTM_L, emit=True, name="mgate")
    state = _mlstm_call(dec_table(dec_c), None, kt_c, k_c, v_c, (ws_cf,), (ws_cb,), None,
                        chunk=MLSTM_L, emit=False, name="mlstm_ctx")
    hf, hb = _mlstm_call(dec_table(dec_l), q_l, kt_l, k_l, v_l, (p_f, rep_f, ws_f), (p_b, rep_b, ws_b),
                         state, chunk=MLSTM_L, emit=True, name="mlstm")

    return _tail_call(hf, hb, o_l, u_l, vs_l, ga_l, gb_l, h, mods, g_head[layer], g_sgu[layer],
                      w_s[layer].astype(BF16), b_s[layer].T, wa, wb, wo,
                      g_ffn2[layer], w2_in, w2_out, g_final, tm=TAIL_TM)
```

```python
import functools
import math

import jax
import jax.numpy as jnp
from jax import lax
from jax.experimental import pallas as pl
from jax.experimental.pallas import tpu as pltpu

F32 = jnp.float32
BF16 = jnp.bfloat16

D_MODEL = 1024
GRID_W = 64
N_HEADS = 4
HEAD_DIM = 256
N_GROUPS = 4
GROUP_DIM = 256
SGU_CHUNK = 128
D_FF = 2816
N_MOD = 9
N_GATES = 16
POS_BASE = 10000.0
EPS = 1e-6

LANES = 128
HALO = 8
VMEM_LIMIT = 56 * 1024 * 1024
MASK_NEG = -1e30

FFN_TM = 512
PROJ_TM = 512
TAIL_TM = 256
MLSTM_L = 256
FF_CHUNKS = ((0, 1024), (1024, 2048), (2048, D_FF))


def _dot(a, b):
    return jnp.dot(a, b, preferred_element_type=F32)


def _dot_t(a, b_t):
    return lax.dot_general(a, b_t, (((1,), (1,)), ((), ())), preferred_element_type=F32)


def _sigmoid(x):
    return 1.0 / (1.0 + jnp.exp(-x))


def _gelu_tanh(x):
    c = math.sqrt(2.0 / math.pi)
    return 0.5 * x * (1.0 + jnp.tanh(c * (x + 0.044715 * (x * x * x))))


def _log_sigmoid(x):
    return jnp.minimum(x, 0.0) - jnp.log(1.0 + jnp.exp(-jnp.abs(x)))


def _rms(x, g):
    return x * lax.rsqrt(jnp.mean(x * x, axis=-1, keepdims=True) + EPS) * g


def _norm_mod(x, g, shift, scale):
    return _rms(x, g) * (1.0 + scale) + shift


def _const_spec(shape):
    zeros = (0,) * len(shape)
    return pl.BlockSpec(shape, lambda *_: zeros, pipeline_mode=pl.Buffered(1))


def _params(sem):
    return pltpu.CompilerParams(dimension_semantics=sem, vmem_limit_bytes=VMEM_LIMIT)


def _mods_kernel(c_ref, w_ref, b_ref, o_ref):
    c = c_ref[...]
    s = (c * _sigmoid(c)).astype(BF16)
    o_ref[...] = _dot(s, w_ref[...].astype(BF16)) + b_ref[...]


def _mods_call(cond, w_ada, b_ada):
    rows, d = cond.shape
    n = w_ada.shape[1]
    tn = 1024
    return pl.pallas_call(
        _mods_kernel,
        out_shape=jax.ShapeDtypeStruct((rows, n), F32),
        grid=(n // tn,),
        in_specs=[
            pl.BlockSpec((rows, d), lambda j: (0, 0)),
            pl.BlockSpec((d, tn), lambda j: (0, j)),
            pl.BlockSpec((1, tn), lambda j: (0, j)),
        ],
        out_specs=pl.BlockSpec((rows, tn), lambda j: (0, j)),
        compiler_params=_params(("arbitrary",)),
        name="mods",
    )(cond, w_ada, b_ada.reshape(1, n))


def _ffn_body(x, m, mod_base, g, win_ref, wout_ref):
    shift = m[mod_base:mod_base + 1]
    scale = m[mod_base + 1:mod_base + 2]
    gate = m[mod_base + 2:mod_base + 3]
    hb = _norm_mod(x, g, shift, scale).astype(BF16)
    y = None
    for c0, c1 in FF_CHUNKS:
        a = _dot(hb, win_ref[:, c0:c1])
        b = _dot(hb, win_ref[:, D_FF + c0:D_FF + c1])
        act = (a * _sigmoid(a) * b).astype(BF16)
        part = _dot(act, wout_ref[c0:c1, :])
        y = part if y is None else y + part
    return x + 0.5 * gate * y


def _ffn_kernel(*refs, tm, mod_base, add_pos, n_cast):
    it = iter(refs)
    x_ref = next(it)
    prow_ref, pcol_ref = (next(it), next(it)) if add_pos else (None, None)
    mods_ref, g_ref, win_ref, wout_ref = (next(it) for _ in range(4))
    cast_in = [next(it) for _ in range(n_cast)]
    o_ref = next(it)
    cast_out = [next(it) for _ in range(n_cast)]

    x = x_ref[0]
    if add_pos:
        pcol = pcol_ref[...]
        half = pcol.shape[1]
        pos = [jnp.concatenate([jnp.broadcast_to(prow_ref[r:r + 1, :], (GRID_W, half)), pcol], axis=-1)
               for r in range(tm // GRID_W)]
        x = x + jnp.concatenate(pos, axis=0)
    o_ref[0] = _ffn_body(x, mods_ref[0], mod_base, g_ref[...], win_ref, wout_ref)

    for src, dst in zip(cast_in, cast_out):
        dst[...] = src[...].astype(BF16)


def _ffn_call(x, pos, mods, mod_row, g, w_in, w_out, casts, *, mod_base, tm, name):
    bsz, s, d = x.shape
    add_pos = pos is not None
    nstep = (s // tm) * bsz
    args = [x]
    specs = [pl.BlockSpec((1, tm, d), lambda i, b: (b, i, 0))]
    if add_pos:
        prow, pcol = pos
        args += [prow, pcol]
        specs += [pl.BlockSpec((tm // GRID_W, prow.shape[1]), lambda i, b: (i, 0)),
                  _const_spec(pcol.shape)]
    args += [mods, g.reshape(1, d), w_in, w_out]
    specs += [
        pl.BlockSpec((1, N_MOD, d), lambda i, b: (mod_row(b), 0, 0)),
        _const_spec((1, d)),
        _const_spec(w_in.shape),
        _const_spec(w_out.shape),
    ]
    out_shape = [jax.ShapeDtypeStruct((bsz, s, d), F32)]
    out_specs = [pl.BlockSpec((1, tm, d), lambda i, b: (b, i, 0))]
    for w, first, count, rows in casts:
        nblk = count // rows
        assert nblk * rows == count and nblk <= nstep
        blk = lambda i, b, nblk=nblk: jnp.minimum(i * bsz + b, nblk - 1)
        args.append(w)
        specs.append(pl.BlockSpec((pl.Element(rows), pl.Element(w.shape[1])),
                                  lambda i, b, blk=blk, first=first, rows=rows:
                                  (pl.multiple_of(first + blk(i, b) * rows, HALO), 0)))
        out_shape.append(jax.ShapeDtypeStruct((count, w.shape[1]), BF16))
        out_specs.append(pl.BlockSpec((rows, w.shape[1]), lambda i, b, blk=blk: (blk(i, b), 0)))
    kern = functools.partial(_ffn_kernel, tm=tm, mod_base=mod_base, add_pos=add_pos, n_cast=len(casts))
    return pl.pallas_call(
        kern,
        out_shape=out_shape,
        grid=(s // tm, bsz),
        in_specs=specs,
        out_specs=out_specs,
        compiler_params=_params(("arbitrary", "arbitrary")),
        name=name,
    )(*args)


def _proj_kernel(*refs, tm, with_rest):
    it = iter(refs)
    h_ref, hp_ref, hx_ref, mods_ref, g_ref = (next(it) for _ in range(5))
    wqk_ref, wv_ref, wgt_ref = (next(it) for _ in range(3))
    wrest_ref = next(it) if with_rest else None
    cw_ref, cb_ref, bgr_ref = (next(it) for _ in range(3))
    q_ref, kt_ref, v_ref, gr_ref = (next(it) for _ in range(4))
    rest_refs = [next(it) for _ in range(5)] if with_rest else []
    raw_ref = next(it)

    i = pl.program_id(0)
    nt = pl.num_programs(0)
    m = mods_ref[0]
    shift, scale = m[3:4], m[4:5]
    g = g_ref[...]
    hn = _norm_mod(h_ref[0], g, shift, scale)
    prev_ok = (i > 0).astype(F32)
    next_ok = (i < nt - 1).astype(F32)
    hn_prev = _norm_mod(hp_ref[0], g, shift, scale) * prev_ok
    hn_next = _norm_mod(hx_ref[0], g, shift, scale) * next_ok
    hb = hn.astype(BF16)
    ext = jnp.concatenate([hn_prev, hn, hn_next], axis=0).astype(BF16)

    raw_ref[...] = _dot_t(ext, wqk_ref[...])
    sub8 = lax.broadcasted_iota(jnp.int32, (HALO, HEAD_DIM), 0)

    def conv_chunk(c):
        cols = slice(c * HEAD_DIM, (c + 1) * HEAD_DIM)
        cw = cw_ref[:, cols]
        x0 = raw_ref[HALO:HALO + tm, cols]
        down = pltpu.roll(x0, 1, axis=0)
        first = jnp.where(sub8 == 0, raw_ref[HALO - 1:HALO, cols], down[:HALO])
        xm = jnp.concatenate([first, down[HALO:]], axis=0)
        up = pltpu.roll(x0, tm - 1, axis=0)
        last = jnp.where(sub8 == HALO - 1, raw_ref[HALO + tm:HALO + tm + 1, cols], up[tm - HALO:])
        xp = jnp.concatenate([up[:tm - HALO], last], axis=0)
        acc = xm * cw[0:1] + x0 * cw[1:2] + xp * cw[2:3] + cb_ref[:, cols]
        act = acc * _sigmoid(acc)
        if c < N_HEADS:
            q_ref[0, :, cols] = (act * HEAD_DIM ** -0.5).astype(BF16)
        else:
            kc = slice((c - N_HEADS) * HEAD_DIM, (c - N_HEADS + 1) * HEAD_DIM)
            kt_ref[0, kc, :] = act.T.astype(BF16)

    def rest_piece(j):
        rest_refs[j][0] = _dot_t(hb, wrest_ref[j * D_MODEL:(j + 1) * D_MODEL, :]).astype(BF16)

    grow = _dot_t(wgt_ref[...], hb) + bgr_ref[...]
    sub = lax.broadcasted_iota(jnp.int32, grow.shape, 0)
    gr_ref[0] = jnp.where(sub % 8 >= N_HEADS, _log_sigmoid(grow), grow)

    v_ref[0] = _dot_t(hb, wv_ref[...]).astype(BF16)
    for c in range(2 * N_HEADS):
        if with_rest and c < 5:
            rest_piece(c)
        conv_chunk(c)


def _proj_call(h, mods, mod_row, g, wqk, wv, wgt, wrest, conv_w, conv_b, bg_row, *, tm, name):
    bsz, s, d = h.shape
    with_rest = wrest is not None
    nhb = tm // HALO
    last = s // HALO - 1
    args = [h, h, h, mods, g.reshape(1, d), wqk, wv, wgt]
    specs = [
        pl.BlockSpec((1, tm, d), lambda i, b: (b, i, 0)),
        pl.BlockSpec((1, HALO, d), lambda i, b: (b, jnp.maximum(i * nhb - 1, 0), 0)),
        pl.BlockSpec((1, HALO, d), lambda i, b: (b, jnp.minimum((i + 1) * nhb, last), 0)),
        pl.BlockSpec((1, N_MOD, d), lambda i, b: (mod_row(b), 0, 0)),
        _const_spec((1, d)),
        _const_spec(wqk.shape), _const_spec(wv.shape), _const_spec(wgt.shape),
    ]
    if with_rest:
        args.append(wrest)
        specs.append(_const_spec(wrest.shape))
    args += [conv_w, conv_b.reshape(1, -1), bg_row]
    specs += [_const_spec(conv_w.shape), _const_spec((1, conv_b.shape[0])), _const_spec(bg_row.shape)]
    tok = lambda i, b: (b, i, 0)
    out_shape = [
        jax.ShapeDtypeStruct((bsz, s, d), BF16),
        jax.ShapeDtypeStruct((bsz, d, s), BF16),
        jax.ShapeDtypeStruct((bsz, s, d), BF16),
        jax.ShapeDtypeStruct((bsz, N_GATES, s), F32),
    ]
    out_specs = [
        pl.BlockSpec((1, tm, d), tok),
        pl.BlockSpec((1, d, tm), lambda i, b: (b, 0, i)),
        pl.BlockSpec((1, tm, d), tok),
        pl.BlockSpec((1, N_GATES, tm), lambda i, b: (b, 0, i)),
    ]
    if with_rest:
        out_shape += [jax.ShapeDtypeStruct((bsz, s, d), BF16)] * 5
        out_specs += [pl.BlockSpec((1, tm, d), tok)] * 5
    kern = functools.partial(_proj_kernel, tm=tm, with_rest=with_rest)
    return pl.pallas_call(
        kern,
        out_shape=out_shape,
        grid=(s // tm, bsz),
        in_specs=specs,
        out_specs=out_specs,
        scratch_shapes=[pltpu.VMEM((tm + 2 * HALO, 2 * d), F32)],
        compiler_params=_params(("arbitrary", "arbitrary")),
        name=name,
    )(*args)


def _split3(x):
    hi = x.astype(BF16)
    r1 = x - hi.astype(F32)
    mid = r1.astype(BF16)
    lo = (r1 - mid.astype(F32)).astype(BF16)
    return hi, mid, lo


def _chain(d, b, hd, nb):
    return (d * nb + b) * N_HEADS + hd


def _mgate_kernel(*refs, chunk, nb, emit, has_init):
    it = iter(refs)
    gr_refs = (next(it), next(it))
    m0_ref = next(it) if has_init else None
    col_refs = (next(it), next(it)) if emit else None
    r_refs = (next(it), next(it)) if emit else None
    rep_refs = (next(it), next(it)) if emit else None
    ws_refs = (next(it), next(it))
    dec_ref = next(it)
    m_ref = next(it)

    @pl.when(pl.program_id(0) == 0)
    def _():
        m_ref[...] = m0_ref[...] if has_init else jnp.zeros_like(m_ref)

    row = lax.broadcasted_iota(jnp.int32, (chunk, chunk), 0)
    col = lax.broadcasted_iota(jnp.int32, (chunk, chunk), 1)
    lane = lax.broadcasted_iota(jnp.int32, (nb * N_HEADS, chunk), 1)
    ones = jnp.ones((chunk, chunk), BF16)
    sel_r = lax.broadcasted_iota(jnp.int32, (LANES, 2 * N_HEADS * LANES), 0)
    sel_j = lax.broadcasted_iota(jnp.int32, (LANES, 2 * N_HEADS * LANES), 1) // LANES
    spread = jnp.where(sel_r == sel_j + N_HEADS, 1.0, 0.0).astype(BF16)

    tri_ones = [jnp.concatenate([jnp.where(keep, 1.0, 0.0).astype(BF16), ones], axis=1)
                for keep in (row <= col, row >= col)]

    li, bq, g4 = [], [], []
    for d, gr_ref in enumerate(gr_refs):
        parts = []
        for b in range(nb):
            gr = gr_ref[b]
            sums = _dot(jnp.concatenate(_split3(gr), axis=0), tri_ones[d])
            sums = sums[:N_GATES] + sums[N_GATES:2 * N_GATES] + sums[2 * N_GATES:]
            b_all = sums[:, :chunk]
            g_all = sums[:, chunk:]
            f_rows = slice(8 * d + N_HEADS, 8 * d + 2 * N_HEADS)
            parts.append((gr[8 * d:8 * d + N_HEADS], b_all[f_rows], g_all[f_rows]))
        li.append(jnp.concatenate([p[0] for p in parts], axis=0))
        bq.append(jnp.concatenate([p[1] for p in parts], axis=0))
        g4.append(jnp.concatenate([p[2] for p in parts], axis=0))

    r, m_st = [], []
    for d in range(2):
        r.append(li[d] - bq[d])
        m_st.append(m_ref[d])
        m_new = g4[d] + jnp.maximum(m_st[d], jnp.max(r[d], axis=-1, keepdims=True))
        ws = jnp.exp(g4[d] + r[d] - m_new)
        for b in range(nb):
            ws_refs[d][b] = ws[b * N_HEADS:(b + 1) * N_HEADS]
        dec_ref[0, d * nb * N_HEADS:(d + 1) * nb * N_HEADS, :] = jnp.exp(g4[d] + m_st[d] - m_new)[:, :LANES]
        m_ref[d] = m_new
    if not emit:
        return

    run = list(r)
    k = 1
    while k < chunk:
        for d in range(2):
            if d == 0:
                shifted = jnp.where(lane >= k, pltpu.roll(run[d], k, axis=1), MASK_NEG)
            else:
                shifted = jnp.where(lane < chunk - k, pltpu.roll(run[d], chunk - k, axis=1), MASK_NEG)
            run[d] = jnp.maximum(run[d], shifted)
        k *= 2

    cols = {}
    for d in range(2):
        m4 = jnp.maximum(m_st[d], run[d])
        inter = jnp.exp(m_st[d] - m4)
        ei = jnp.exp(-bq[d] - m4)
        for b in range(nb):
            sl = slice(b * N_HEADS, (b + 1) * N_HEADS)
            rows = jnp.concatenate([m4[sl], inter[sl], ei[sl],
                                    jnp.zeros((LANES - 3 * N_HEADS, chunk), F32)], axis=0)
            cols[d, b] = rows.T
    for (d, b), c in cols.items():
        rep_refs[d][b] = _dot(c.astype(BF16), spread).astype(BF16)
    for (d, b), c in cols.items():
        col_refs[d][b] = c
        r_refs[d][b] = r[d][b * N_HEADS:(b + 1) * N_HEADS]


def _mgate_call(gr, m0, *, chunk, emit, name):
    bsz, _, s = gr.shape
    nc = s // chunk
    ngroup = 2 * bsz
    has_init = m0 is not None
    fwd = lambda i: i
    bwd = lambda i: nc - 1 - i
    args = [gr, gr]
    specs = [pl.BlockSpec((bsz, N_GATES, chunk), lambda i: (0, 0, fwd(i))),
             pl.BlockSpec((bsz, N_GATES, chunk), lambda i: (0, 0, bwd(i)))]
    m_shape = (2, bsz * N_HEADS, chunk)
    if has_init:
        args.append(m0)
        specs.append(_const_spec(m_shape))
    out_shape, out_specs = [], []
    if emit:
        out_shape += [jax.ShapeDtypeStruct((bsz, s, LANES), F32)] * 2
        out_specs += [pl.BlockSpec((bsz, chunk, LANES), lambda i: (0, fwd(i), 0)),
                      pl.BlockSpec((bsz, chunk, LANES), lambda i: (0, bwd(i), 0))]
        out_shape += [jax.ShapeDtypeStruct((bsz, N_HEADS, s), F32)] * 2
        out_specs += [pl.BlockSpec((bsz, N_HEADS, chunk), lambda i: (0, 0, fwd(i))),
                      pl.BlockSpec((bsz, N_HEADS, chunk), lambda i: (0, 0, bwd(i)))]
        out_shape += [jax.ShapeDtypeStruct((bsz, s, 2 * N_HEADS * LANES), BF16)] * 2
        out_specs += [pl.BlockSpec((bsz, chunk, 2 * N_HEADS * LANES), lambda i: (0, fwd(i), 0)),
                      pl.BlockSpec((bsz, chunk, 2 * N_HEADS * LANES), lambda i: (0, bwd(i), 0))]
    out_shape += [jax.ShapeDtypeStruct((bsz, N_HEADS, s), F32)] * 2
    out_specs += [pl.BlockSpec((bsz, N_HEADS, chunk), lambda i: (0, 0, fwd(i))),
                  pl.BlockSpec((bsz, N_HEADS, chunk), lambda i: (0, 0, bwd(i)))]
    out_shape.append(jax.ShapeDtypeStruct((nc, ngroup * N_HEADS, LANES), F32))
    out_specs.append(pl.BlockSpec((1, ngroup * N_HEADS, LANES), lambda i: (i, 0, 0)))
    if emit:
        scratch = [pltpu.VMEM(m_shape, F32)]
    else:
        out_shape.append(jax.ShapeDtypeStruct(m_shape, F32))
        out_specs.append(pl.BlockSpec(m_shape, lambda i: (0, 0, 0)))
        scratch = []
    kern = functools.partial(_mgate_kernel, chunk=chunk, nb=bsz, emit=emit, has_init=has_init)
    return pl.pallas_call(
        kern,
        out_shape=out_shape,
        grid=(nc,),
        in_specs=specs,
        out_specs=out_specs,
        scratch_shapes=scratch,
        compiler_params=_params(("arbitrary",)),
        name=name,
    )(*args)


def _fold_lanes(x):
    blocks = [x[:, c:c + LANES] for c in range(0, x.shape[1], LANES)]
    return functools.reduce(lambda a, b: a + b, blocks)


def _tile_lanes(x, width):
    return jnp.concatenate([x] * (width // LANES), axis=-1)


def _mlstm_kernel(*refs, chunk, nb, emit, has_init):
    it = iter(refs)
    dec_ref = next(it)
    n_in = 7 if emit else 3
    dirs = [tuple(next(it) for _ in range(n_in)) for _ in range(2)]
    init = (next(it), next(it)) if has_init else None
    h_refs = (next(it), next(it)) if emit else None
    c_ref, n_ref = next(it), next(it)

    step = pl.program_id(0)
    nchain = 2 * nb * N_HEADS

    @pl.when(step == 0)
    def _():
        if has_init:
            c_ref[...] = init[0][...]
            n_ref[...] = init[1][...]
        else:
            c_ref[...] = jnp.zeros_like(c_ref)
            n_ref[...] = jnp.zeros_like(n_ref)

    row = lax.broadcasted_iota(jnp.int32, (chunk, chunk), 0)
    col = lax.broadcasted_iota(jnp.int32, (chunk, chunk), 1)
    visible = (row >= col, row <= col)

    for d, dir_refs in enumerate(dirs):
        if emit:
            q_ref, kt_ref, v_ref, col_ref, r_ref, rep_ref, ws_ref = dir_refs
        else:
            kt_ref, v_ref, ws_ref = dir_refs
        for b in range(nb):
            for hd in range(N_HEADS):
                j = _chain(d, b, hd, nb)
                lanes = slice(hd * HEAD_DIM, (hd + 1) * HEAD_DIM)
                decay = dec_ref[step * nchain + j]
                ws = ws_ref[b, hd:hd + 1, :].astype(BF16)
                kt = kt_ref[b, lanes, :]
                v = v_ref[b, :, lanes]
                ct = c_ref[j]
                n_st = n_ref[j]
                if emit:
                    q = q_ref[b, :, lanes]
                    inter = rep_ref[b, :, hd * LANES:(hd + 1) * LANES].astype(F32)
                    ei = rep_ref[b, :, (N_HEADS + hd) * LANES:(N_HEADS + hd + 1) * LANES].astype(F32)
                    p = jnp.where(visible[d], jnp.exp(r_ref[b, hd:hd + 1, :] - col_ref[b, :, hd:hd + 1]), 0.0)
                    s_ts = _dot(q, kt) * p
                    num_intra = _dot(s_ts.astype(BF16), v)
                    num_inter = _dot(q, ct.astype(BF16))
                    den = jnp.sum(_fold_lanes(s_ts) + inter * _fold_lanes(q.astype(F32) * n_st),
                                  axis=-1, keepdims=True)
                    rden = 1.0 / jnp.maximum(jnp.abs(den), ei)
                    h = (num_intra + _tile_lanes(inter, HEAD_DIM) * num_inter) * _tile_lanes(rden, HEAD_DIM)
                    h_refs[d][b, :, lanes] = h.astype(BF16)
                c_ref[j] = decay * ct + _dot(kt * ws, v)
                n_ref[j] = decay * n_st + _dot_t(ws, kt)


def _mlstm_call(dec, q, kt, v, gate_f, gate_b, init, *, chunk, emit, name):
    bsz, d, s = kt.shape
    nc = s // chunk
    nchain = 2 * bsz * N_HEADS
    has_init = init is not None

    def dir_args(gate, idx):
        tok = lambda i: (0, idx(i), 0)
        feat = lambda i: (0, 0, idx(i))
        args, specs = [], []
        if emit:
            args.append(q)
            specs.append(pl.BlockSpec((bsz, chunk, d), tok))
        args += [kt, v]
        specs += [pl.BlockSpec((bsz, d, chunk), feat), pl.BlockSpec((bsz, chunk, d), tok)]
        if emit:
            cols, r, rep, ws = gate
            args += [cols, r, rep]
            specs += [pl.BlockSpec((bsz, chunk, LANES), tok),
                      pl.BlockSpec((bsz, N_HEADS, chunk), feat),
                      pl.BlockSpec((bsz, chunk, 2 * N_HEADS * LANES), tok)]
        else:
            (ws,) = gate
        args.append(ws)
        specs.append(pl.BlockSpec((bsz, N_HEADS, chunk), feat))
        return args, specs

    fwd = lambda i: i
    bwd = lambda i: nc - 1 - i
    af, sf = dir_args(gate_f, fwd)
    ab, sb = dir_args(gate_b, bwd)
    args = [dec] + af + ab
    specs = [pl.BlockSpec(memory_space=pltpu.SMEM)] + sf + sb
    state_shapes = [(nchain, HEAD_DIM, HEAD_DIM), (nchain, 1, HEAD_DIM)]
    if has_init:
        args += list(init)
        specs += [_const_spec(sh) for sh in state_shapes]
    kern = functools.partial(_mlstm_kernel, chunk=chunk, nb=bsz, emit=emit, has_init=has_init)
    if emit:
        out_shape = [jax.ShapeDtypeStruct((bsz, s, d), BF16)] * 2
        out_specs = [pl.BlockSpec((bsz, chunk, d), lambda i: (0, fwd(i), 0)),
                     pl.BlockSpec((bsz, chunk, d), lambda i: (0, bwd(i), 0))]
        scratch = [pltpu.VMEM(sh, F32) for sh in state_shapes]
    else:
        out_shape = [jax.ShapeDtypeStruct(sh, F32) for sh in state_shapes]
        out_specs = [pl.BlockSpec(sh, lambda i: (0, 0, 0)) for sh in state_shapes]
        scratch = []
    return pl.pallas_call(
        kern,
        out_shape=out_shape,
        grid=(nc,),
        in_specs=specs,
        out_specs=out_specs,
        scratch_shapes=scratch,
        compiler_params=_params(("arbitrary",)),
        name=name,
    )(*args)


def _tail_kernel(hf_ref, hb_ref, o_ref, u_ref, vs_ref, ga_ref, gb_ref, h_ref, mods_ref, mods_ffn_ref,
                 ghead_ref, gsgu_ref, ws_ref, bs_ref, wa_ref, wb_ref, wo_ref,
                 g2_ref, win_ref, wout_ref, gfin_ref, out_ref, yb_ref, h2_ref, *, tm):
    @pl.when(pl.program_id(0) == 0)
    def _():
        h2_ref[...] = jnp.zeros_like(h2_ref)

    h2_prev = h2_ref[...]
    out_ref[0] = _rms(_ffn_body(h2_prev, mods_ffn_ref[0], 6, g2_ref[...], win_ref, wout_ref), gfin_ref[...])

    m = mods_ref[0]
    hm = hf_ref[0].astype(F32) + hb_ref[0].astype(F32)
    parts = []
    for hd in range(N_HEADS):
        x = hm[:, hd * HEAD_DIM:(hd + 1) * HEAD_DIM]
        xc = x - jnp.mean(x, axis=-1, keepdims=True)
        parts.append(xc * lax.rsqrt(jnp.mean(xc * xc, axis=-1, keepdims=True) + EPS))
    ln = jnp.concatenate(parts, axis=-1) * ghead_ref[...]
    y_a = (_sigmoid(o_ref[0].astype(F32)) * ln).astype(BF16)

    vn = _rms(_gelu_tanh(vs_ref[0].astype(F32)), gsgu_ref[...]).astype(BF16)
    for r in range(tm // SGU_CHUNK):
        rows = slice(r * SGU_CHUNK, (r + 1) * SGU_CHUNK)
        for gidx in range(N_GROUPS):
            cols = slice(gidx * GROUP_DIM, (gidx + 1) * GROUP_DIM)
            mixed = _dot(ws_ref[gidx], vn[rows, cols]) + bs_ref[:, gidx:gidx + 1]
            gu = _gelu_tanh(u_ref[0, rows, cols].astype(F32))
            yb_ref[rows, cols] = (gu * mixed).astype(BF16)

    mixed_out = (_sigmoid(ga_ref[0].astype(F32)) * _dot(y_a, wa_ref[...])
                 + _sigmoid(gb_ref[0].astype(F32)) * _dot(yb_ref[...], wb_ref[...]))
    y = _dot(mixed_out.astype(BF16), wo_ref[...])
    h2_ref[...] = h_ref[0] + m[5:6] * y


def _tail_call(hf, hb, o, u, vs, ga, gb, h, mods, g_head, g_sgu, w_s, b_s_t, w_a, w_b, w_o,
               g2, w_in, w_out, g_final, *, tm):
    bsz, s, d = h.shape
    ntile = bsz * (s // tm)
    merge_tile = lambda t: jnp.minimum(t, ntile - 1)
    ffn_tile = lambda t: jnp.maximum(t - 1, 0)
    tok = pl.BlockSpec((1, tm, d), lambda t: (merge_tile(t) % bsz, merge_tile(t) // bsz, 0))
    vec = _const_spec((1, d))
    specs = [tok] * 8 + [
        pl.BlockSpec((1, N_MOD, d), lambda t: (merge_tile(t) % bsz, 0, 0)),
        pl.BlockSpec((1, N_MOD, d), lambda t: (ffn_tile(t) % bsz, 0, 0)),
        vec, vec, _const_spec(w_s.shape), _const_spec(b_s_t.shape),
        _const_spec(w_a.shape), _const_spec(w_b.shape), _const_spec(w_o.shape),
        vec, _const_spec(w_in.shape), _const_spec(w_out.shape), vec,
    ]
    return pl.pallas_call(
        functools.partial(_tail_kernel, tm=tm),
        out_shape=jax.ShapeDtypeStruct((bsz, s, d), F32),
        grid=(ntile + 1,),
        in_specs=specs,
        out_specs=pl.BlockSpec((1, tm, d), lambda t: (ffn_tile(t) % bsz, ffn_tile(t) // bsz, 0)),
        scratch_shapes=[pltpu.VMEM((tm, d), BF16), pltpu.VMEM((tm, d), F32)],
        compiler_params=_params(("arbitrary",)),
        name="tail",
    )(hf, hb, o, u, vs, ga, gb, h, mods, mods, g_head.reshape(1, d), g_sgu.reshape(1, d),
      w_s, b_s_t, w_a, w_b, w_o, g2.reshape(1, d), w_in, w_out, g_final.reshape(1, d))


def _pos_tables(rows):
    quarter = D_MODEL // 4
    freqs = jnp.exp(-math.log(POS_BASE) * jnp.arange(quarter, dtype=F32) / quarter)
    ar = jnp.arange(rows, dtype=F32)[:, None] * freqs
    ac = jnp.arange(GRID_W, dtype=F32)[:, None] * freqs
    return (jnp.concatenate([jnp.sin(ar), jnp.cos(ar)], axis=-1),
            jnp.concatenate([jnp.sin(ac), jnp.cos(ac)], axis=-1))


def kernel(x, c, ctx, c_ctx, w_ada, b_ada, g_ffn1, w_ffn1_in, w_ffn1_out, g_mix, w_in, b_gates,
           conv_qk_w, conv_qk_b, g_head, g_sgu, w_s, b_s, w_branch_a, w_branch_b, w_out,
           g_ffn2, w_ffn2_in, w_ffn2_out, g_final):
    bsz, seq, d = x.shape
    ctx_len = ctx.shape[1]
    layer = 0
    pos = _pos_tables(seq // GRID_W)

    cond = jnp.concatenate([c, c_ctx[None], jnp.zeros((HALO - bsz - 1, d), F32)], axis=0)
    mods = _mods_call(cond, w_ada[layer], b_ada[layer]).reshape(HALO, N_MOD, d)
    lat_row = lambda b: b
    ctx_row = lambda b: bsz

    w1_in = w_ffn1_in[layer].astype(BF16)
    w1_out = w_ffn1_out[layer].astype(BF16)
    wi_t = jnp.swapaxes(w_in[layer], 0, 1)
    wgt = wi_t[3 * d:3 * d + N_GATES].astype(BF16)
    bg = b_gates[layer]
    nstep = (seq // FFN_TM) * bsz
    n_rest = wi_t.shape[0] - 3 * d - N_GATES
    casts = [
        (wi_t, 0, 2 * d, 2 * d // nstep),
        (wi_t, 2 * d, d, d // nstep),
        (wi_t, 3 * d + N_GATES, n_rest, n_rest // nstep),
        (w_ffn2_in[layer], 0, d, d // nstep),
        (w_ffn2_out[layer], 0, D_FF, 2 * D_FF // nstep),
        (w_branch_a[layer], 0, d, d // nstep),
        (w_branch_b[layer], 0, d, d // nstep),
        (w_out[layer], 0, d, d // nstep),
    ]

    def bg_row(tm):
        return jnp.broadcast_to(bg[:, None], (N_GATES, tm))

    def dec_table(dec):
        return dec[:, :, 0].reshape(-1)

    h, wqk, wv, wrest, w2_in, w2_out, wa, wb, wo = _ffn_call(
        x, pos, mods, lat_row, g_ffn1[layer], w1_in, w1_out, casts, mod_base=0, tm=FFN_TM, name="ffn1")
    (hc,) = _ffn_call(ctx, None, mods, ctx_row, g_ffn1[layer], w1_in, w1_out, [],
                      mod_base=0, tm=ctx_len, name="ffn1_ctx")

    _, kt_c, v_c, gr_c = _proj_call(
        hc, mods, ctx_row, g_mix[layer], wqk, wv, wgt, None, conv_qk_w[layer], conv_qk_b[layer],
        bg_row(ctx_len), tm=ctx_len, name="proj_ctx")
    q_l, kt_l, v_l, gr_l, o_l, u_l, vs_l, ga_l, gb_l = _proj_call(
        h, mods, lat_row, g_mix[layer], wqk, wv, wgt, wrest, conv_qk_w[layer], conv_qk_b[layer],
        bg_row(PROJ_TM), tm=PROJ_TM, name="proj")

    ws_cf, ws_cb, dec_c, m_c = _mgate_call(gr_c, None, chunk=MLSTM_L, emit=False, name="mgate_ctx")
    col_f, col_b, r_f, r_b, rep_f, rep_b, ws_f, ws_b, dec_l = _mgate_call(
        gr_l, m_c, chunk=MLSTM_L, emit=True, name="mgate")
    state = _mlstm_call(dec_table(dec_c), None, kt_c, v_c, (ws_cf,), (ws_cb,), None,
                        chunk=MLSTM_L, emit=False, name="mlstm_ctx")
    hf, hb = _mlstm_call(dec_table(dec_l), q_l, kt_l, v_l, (col_f, r_f, rep_f, ws_f), (col_b, r_b, rep_b, ws_b),
                         state, chunk=MLSTM_L, emit=True, name="mlstm")

    return _tail_call(hf, hb, o_l, u_l, vs_l, ga_l, gb_l, h, mods, g_head[layer], g_sgu[layer],
                      w_s[layer].astype(BF16), b_s[layer].T, wa, wb, wo,
                      g_ffn2[layer], w2_in, w2_out, g_final, tm=TAIL_TM)
```

```python
import functools
import math

import jax
import jax.numpy as jnp
from jax import lax
from jax.experimental import pallas as pl
from jax.experimental.pallas import tpu as pltpu

F32 = jnp.float32
BF16 = jnp.bfloat16

D_MODEL = 1024
GRID_W = 64
N_HEADS = 4
HEAD_DIM = 256
N_GROUPS = 4
GROUP_DIM = 256
SGU_CHUNK = 128
D_FF = 2816
N_MOD = 9
N_GATES = 16
POS_BASE = 10000.0
EPS = 1e-6

LANES = 128
HALO = 8
VMEM_LIMIT = 56 * 1024 * 1024
MASK_NEG = -1e30

FFN_TM = 512
PROJ_TM = 512
TAIL_TM = 256
MLSTM_L = 256
MGATE_PER = 4
FF_CHUNKS = ((0, 1024), (1024, 2048), (2048, D_FF))


def _dot(a, b):
    return jnp.dot(a, b, preferred_element_type=F32)


def _dot_t(a, b_t):
    return lax.dot_general(a, b_t, (((1,), (1,)), ((), ())), preferred_element_type=F32)


def _sigmoid(x):
    return 1.0 / (1.0 + jnp.exp(-x))


def _gelu_tanh(x):
    c = math.sqrt(2.0 / math.pi)
    return 0.5 * x * (1.0 + jnp.tanh(c * (x + 0.044715 * (x * x * x))))


def _log_sigmoid(x):
    return jnp.minimum(x, 0.0) - jnp.log(1.0 + jnp.exp(-jnp.abs(x)))


def _rms(x, g):
    return x * lax.rsqrt(jnp.mean(x * x, axis=-1, keepdims=True) + EPS) * g


def _norm_mod(x, g, shift, scale):
    return _rms(x, g) * (1.0 + scale) + shift


def _const_spec(shape):
    zeros = (0,) * len(shape)
    return pl.BlockSpec(shape, lambda *_: zeros, pipeline_mode=pl.Buffered(1))


def _params(sem):
    return pltpu.CompilerParams(dimension_semantics=sem, vmem_limit_bytes=VMEM_LIMIT)


def _mods_kernel(c_ref, w_ref, b_ref, o_ref):
    c = c_ref[...]
    s = (c * _sigmoid(c)).astype(BF16)
    o_ref[...] = _dot(s, w_ref[...].astype(BF16)) + b_ref[...]


def _mods_call(cond, w_ada, b_ada):
    rows, d = cond.shape
    n = w_ada.shape[1]
    tn = 1024
    return pl.pallas_call(
        _mods_kernel,
        out_shape=jax.ShapeDtypeStruct((rows, n), F32),
        grid=(n // tn,),
        in_specs=[
            pl.BlockSpec((rows, d), lambda j: (0, 0)),
            pl.BlockSpec((d, tn), lambda j: (0, j)),
            pl.BlockSpec((1, tn), lambda j: (0, j)),
        ],
        out_specs=pl.BlockSpec((rows, tn), lambda j: (0, j)),
        compiler_params=_params(("arbitrary",)),
        name="mods",
    )(cond, w_ada, b_ada.reshape(1, n))


def _ffn_body(x, m, mod_base, g, win_ref, wout_ref):
    shift = m[mod_base:mod_base + 1]
    scale = m[mod_base + 1:mod_base + 2]
    gate = m[mod_base + 2:mod_base + 3]
    hb = _norm_mod(x, g, shift, scale).astype(BF16)
    y = None
    for c0, c1 in FF_CHUNKS:
        a = _dot(hb, win_ref[:, c0:c1])
        b = _dot(hb, win_ref[:, D_FF + c0:D_FF + c1])
        act = (a * _sigmoid(a) * b).astype(BF16)
        part = _dot(act, wout_ref[c0:c1, :])
        y = part if y is None else y + part
    return x + 0.5 * gate * y


def _ffn_kernel(*refs, tm, mod_base, add_pos, n_cast):
    it = iter(refs)
    x_ref = next(it)
    prow_ref, pcol_ref = (next(it), next(it)) if add_pos else (None, None)
    mods_ref, g_ref, win_ref, wout_ref = (next(it) for _ in range(4))
    cast_in = [next(it) for _ in range(n_cast)]
    o_ref = next(it)
    cast_out = [next(it) for _ in range(n_cast)]

    x = x_ref[0]
    if add_pos:
        pcol = pcol_ref[...]
        half = pcol.shape[1]
        pos = [jnp.concatenate([jnp.broadcast_to(prow_ref[r:r + 1, :], (GRID_W, half)), pcol], axis=-1)
               for r in range(tm // GRID_W)]
        x = x + jnp.concatenate(pos, axis=0)
    o_ref[0] = _ffn_body(x, mods_ref[0], mod_base, g_ref[...], win_ref, wout_ref)

    for src, dst in zip(cast_in, cast_out):
        dst[...] = src[...].astype(BF16)


def _ffn_call(x, pos, mods, mod_row, g, w_in, w_out, casts, *, mod_base, tm, name):
    bsz, s, d = x.shape
    add_pos = pos is not None
    nstep = (s // tm) * bsz
    args = [x]
    specs = [pl.BlockSpec((1, tm, d), lambda i, b: (b, i, 0))]
    if add_pos:
        prow, pcol = pos
        args += [prow, pcol]
        specs += [pl.BlockSpec((tm // GRID_W, prow.shape[1]), lambda i, b: (i, 0)),
                  _const_spec(pcol.shape)]
    args += [mods, g.reshape(1, d), w_in, w_out]
    specs += [
        pl.BlockSpec((1, N_MOD, d), lambda i, b: (mod_row(b), 0, 0)),
        _const_spec((1, d)),
        _const_spec(w_in.shape),
        _const_spec(w_out.shape),
    ]
    out_shape = [jax.ShapeDtypeStruct((bsz, s, d), F32)]
    out_specs = [pl.BlockSpec((1, tm, d), lambda i, b: (b, i, 0))]
    for w, first, count, rows in casts:
        nblk = count // rows
        assert nblk * rows == count and nblk <= nstep
        blk = lambda i, b, nblk=nblk: jnp.minimum(i * bsz + b, nblk - 1)
        args.append(w)
        specs.append(pl.BlockSpec((pl.Element(rows), pl.Element(w.shape[1])),
                                  lambda i, b, blk=blk, first=first, rows=rows:
                                  (pl.multiple_of(first + blk(i, b) * rows, HALO), 0)))
        out_shape.append(jax.ShapeDtypeStruct((count, w.shape[1]), BF16))
        out_specs.append(pl.BlockSpec((rows, w.shape[1]), lambda i, b, blk=blk: (blk(i, b), 0)))
    kern = functools.partial(_ffn_kernel, tm=tm, mod_base=mod_base, add_pos=add_pos, n_cast=len(casts))
    return pl.pallas_call(
        kern,
        out_shape=out_shape,
        grid=(s // tm, bsz),
        in_specs=specs,
        out_specs=out_specs,
        compiler_params=_params(("arbitrary", "arbitrary")),
        name=name,
    )(*args)


def _proj_kernel(*refs, tm, with_rest):
    it = iter(refs)
    h_ref, hp_ref, hx_ref, mods_ref, g_ref = (next(it) for _ in range(5))
    wqk_ref, wv_ref, wgt_ref = (next(it) for _ in range(3))
    wrest_ref = next(it) if with_rest else None
    cw_ref, cb_ref, bgr_ref = (next(it) for _ in range(3))
    q_ref, kt_ref, v_ref, gr_ref = (next(it) for _ in range(4))
    rest_refs = [next(it) for _ in range(5)] if with_rest else []
    raw_ref = next(it)

    i = pl.program_id(0)
    nt = pl.num_programs(0)
    m = mods_ref[0]
    shift, scale = m[3:4], m[4:5]
    g = g_ref[...]
    hn = _norm_mod(h_ref[0], g, shift, scale)
    prev_ok = (i > 0).astype(F32)
    next_ok = (i < nt - 1).astype(F32)
    hn_prev = _norm_mod(hp_ref[0], g, shift, scale) * prev_ok
    hn_next = _norm_mod(hx_ref[0], g, shift, scale) * next_ok
    hb = hn.astype(BF16)
    ext = jnp.concatenate([hn_prev, hn, hn_next], axis=0).astype(BF16)

    raw_ref[...] = _dot_t(ext, wqk_ref[...])
    sub8 = lax.broadcasted_iota(jnp.int32, (HALO, HEAD_DIM), 0)

    def conv_chunk(c):
        cols = slice(c * HEAD_DIM, (c + 1) * HEAD_DIM)
        cw = cw_ref[:, cols]
        x0 = raw_ref[HALO:HALO + tm, cols]
        down = pltpu.roll(x0, 1, axis=0)
        first = jnp.where(sub8 == 0, raw_ref[HALO - 1:HALO, cols], down[:HALO])
        xm = jnp.concatenate([first, down[HALO:]], axis=0)
        up = pltpu.roll(x0, tm - 1, axis=0)
        last = jnp.where(sub8 == HALO - 1, raw_ref[HALO + tm:HALO + tm + 1, cols], up[tm - HALO:])
        xp = jnp.concatenate([up[:tm - HALO], last], axis=0)
        acc = xm * cw[0:1] + x0 * cw[1:2] + xp * cw[2:3] + cb_ref[:, cols]
        act = acc * _sigmoid(acc)
        if c < N_HEADS:
            q_ref[0, :, cols] = (act * HEAD_DIM ** -0.5).astype(BF16)
        else:
            kc = slice((c - N_HEADS) * HEAD_DIM, (c - N_HEADS + 1) * HEAD_DIM)
            act_t = act.T.astype(BF16)
            for j in range(tm // MLSTM_L):
                kt_ref[0, j, kc, :] = act_t[:, j * MLSTM_L:(j + 1) * MLSTM_L]

    def rest_piece(j):
        rest_refs[j][0] = _dot_t(hb, wrest_ref[j * D_MODEL:(j + 1) * D_MODEL, :]).astype(BF16)

    grow = _dot_t(wgt_ref[...], hb) + bgr_ref[...]
    sub = lax.broadcasted_iota(jnp.int32, grow.shape, 0)
    gr_ref[0] = jnp.where(sub % 8 >= N_HEADS, _log_sigmoid(grow), grow)

    v_ref[0] = _dot_t(hb, wv_ref[...]).astype(BF16)
    for c in range(2 * N_HEADS):
        if with_rest and c < 5:
            rest_piece(c)
        conv_chunk(c)


def _proj_call(h, mods, mod_row, g, wqk, wv, wgt, wrest, conv_w, conv_b, bg_row, *, tm, name):
    bsz, s, d = h.shape
    with_rest = wrest is not None
    nhb = tm // HALO
    last = s // HALO - 1
    args = [h, h, h, mods, g.reshape(1, d), wqk, wv, wgt]
    specs = [
        pl.BlockSpec((1, tm, d), lambda i, b: (b, i, 0)),
        pl.BlockSpec((1, HALO, d), lambda i, b: (b, jnp.maximum(i * nhb - 1, 0), 0)),
        pl.BlockSpec((1, HALO, d), lambda i, b: (b, jnp.minimum((i + 1) * nhb, last), 0)),
        pl.BlockSpec((1, N_MOD, d), lambda i, b: (mod_row(b), 0, 0)),
        _const_spec((1, d)),
        _const_spec(wqk.shape), _const_spec(wv.shape), _const_spec(wgt.shape),
    ]
    if with_rest:
        args.append(wrest)
        specs.append(_const_spec(wrest.shape))
    args += [conv_w, conv_b.reshape(1, -1), bg_row]
    specs += [_const_spec(conv_w.shape), _const_spec((1, conv_b.shape[0])), _const_spec(bg_row.shape)]
    tok = lambda i, b: (b, i, 0)
    out_shape = [
        jax.ShapeDtypeStruct((bsz, s, d), BF16),
        jax.ShapeDtypeStruct((bsz, s // MLSTM_L, d, MLSTM_L), BF16),
        jax.ShapeDtypeStruct((bsz, s, d), BF16),
        jax.ShapeDtypeStruct((bsz, N_GATES, s), F32),
    ]
    out_specs = [
        pl.BlockSpec((1, tm, d), tok),
        pl.BlockSpec((1, tm // MLSTM_L, d, MLSTM_L), lambda i, b: (b, i, 0, 0)),
        pl.BlockSpec((1, tm, d), tok),
        pl.BlockSpec((1, N_GATES, tm), lambda i, b: (b, 0, i)),
    ]
    if with_rest:
        out_shape += [jax.ShapeDtypeStruct((bsz, s, d), BF16)] * 5
        out_specs += [pl.BlockSpec((1, tm, d), tok)] * 5
    kern = functools.partial(_proj_kernel, tm=tm, with_rest=with_rest)
    return pl.pallas_call(
        kern,
        out_shape=out_shape,
        grid=(s // tm, bsz),
        in_specs=specs,
        out_specs=out_specs,
        scratch_shapes=[pltpu.VMEM((tm + 2 * HALO, 2 * d), F32)],
        compiler_params=_params(("arbitrary", "arbitrary")),
        name=name,
    )(*args)


def _split3(x):
    hi = x.astype(BF16)
    r1 = x - hi.astype(F32)
    mid = r1.astype(BF16)
    lo = (r1 - mid.astype(F32)).astype(BF16)
    return hi, mid, lo


def _chain(d, b, hd, nb):
    return (d * nb + b) * N_HEADS + hd


def _mgate_kernel(*refs, chunk, per, nb, emit, has_init):
    it = iter(refs)
    gr_refs = (next(it), next(it))
    m0_ref = next(it) if has_init else None
    col_refs = (next(it), next(it)) if emit else None
    r_refs = (next(it), next(it)) if emit else None
    rep_refs = (next(it), next(it)) if emit else None
    ws_refs = (next(it), next(it))
    dec_ref = next(it)
    m_ref = next(it)

    @pl.when(pl.program_id(0) == 0)
    def _():
        m_ref[...] = m0_ref[...] if has_init else jnp.zeros_like(m_ref)

    row = lax.broadcasted_iota(jnp.int32, (chunk, chunk), 0)
    col = lax.broadcasted_iota(jnp.int32, (chunk, chunk), 1)
    lane = lax.broadcasted_iota(jnp.int32, (nb * N_HEADS, chunk), 1)
    ones = jnp.ones((chunk, chunk), BF16)
    sel_r = lax.broadcasted_iota(jnp.int32, (LANES, 2 * N_HEADS * LANES), 0)
    sel_j = lax.broadcasted_iota(jnp.int32, (LANES, 2 * N_HEADS * LANES), 1) // LANES
    spread = jnp.where(sel_r == sel_j + N_HEADS, 1.0, 0.0).astype(BF16)

    tri_ones = [jnp.concatenate([jnp.where(keep, 1.0, 0.0).astype(BF16), ones], axis=1)
                for keep in (row <= col, row >= col)]

    units = [(d, sub) for d in range(2) for sub in range(per)]

    def lanes_of(d, sub):
        first = sub * chunk if d == 0 else (per - 1 - sub) * chunk
        return slice(first, first + chunk)

    li, bq, g4 = {}, {}, {}
    for d, sub in units:
        parts = []
        for b in range(nb):
            gr = gr_refs[d][b, :, lanes_of(d, sub)]
            sums = _dot(jnp.concatenate(_split3(gr), axis=0), tri_ones[d])
            sums = sums[:N_GATES] + sums[N_GATES:2 * N_GATES] + sums[2 * N_GATES:]
            b_all = sums[:, :chunk]
            g_all = sums[:, chunk:]
            f_rows = slice(8 * d + N_HEADS, 8 * d + 2 * N_HEADS)
            parts.append((gr[8 * d:8 * d + N_HEADS], b_all[f_rows], g_all[f_rows]))
        li[d, sub] = jnp.concatenate([p[0] for p in parts], axis=0)
        bq[d, sub] = jnp.concatenate([p[1] for p in parts], axis=0)
        g4[d, sub] = jnp.concatenate([p[2] for p in parts], axis=0)

    r = {u: li[u] - bq[u] for u in units}
    m_st = {}
    for d in range(2):
        m = m_ref[d]
        for sub in range(per):
            u = (d, sub)
            m_st[u] = m
            m = g4[u] + jnp.maximum(m, jnp.max(r[u], axis=-1, keepdims=True))
            ws = jnp.exp(g4[u] + r[u] - m)
            for b in range(nb):
                ws_refs[d][b, :, lanes_of(d, sub)] = ws[b * N_HEADS:(b + 1) * N_HEADS]
            dec_ref[sub, d * nb * N_HEADS:(d + 1) * nb * N_HEADS, :] = jnp.exp(g4[u] + m_st[u] - m)[:, :LANES]
        m_ref[d] = m
    if not emit:
        return

    run = dict(r)
    k = 1
    while k < chunk:
        for u in units:
            if u[0] == 0:
                shifted = jnp.where(lane >= k, pltpu.roll(run[u], k, axis=1), MASK_NEG)
            else:
                shifted = jnp.where(lane < chunk - k, pltpu.roll(run[u], chunk - k, axis=1), MASK_NEG)
            run[u] = jnp.maximum(run[u], shifted)
        k *= 2

    cols = {}
    for u in units:
        m4 = jnp.maximum(m_st[u], run[u])
        inter = jnp.exp(m_st[u] - m4)
        ei = jnp.exp(-bq[u] - m4)
        for b in range(nb):
            sl = slice(b * N_HEADS, (b + 1) * N_HEADS)
            rows = jnp.concatenate([m4[sl], inter[sl], ei[sl],
                                    jnp.zeros((LANES - 3 * N_HEADS, chunk), F32)], axis=0)
            cols[u, b] = rows.T
    for ((d, sub), b), c in cols.items():
        rep_refs[d][b, lanes_of(d, sub), :] = _dot(c.astype(BF16), spread).astype(BF16)
    for ((d, sub), b), c in cols.items():
        col_refs[d][b, lanes_of(d, sub), :] = c
        r_refs[d][b, :, lanes_of(d, sub)] = r[d, sub][b * N_HEADS:(b + 1) * N_HEADS]


def _mgate_call(gr, m0, *, chunk, per, emit, name):
    bsz, _, s = gr.shape
    nc = s // chunk
    nstep = nc // per
    assert nstep * per == nc
    span = per * chunk
    ngroup = 2 * bsz
    has_init = m0 is not None
    fwd = lambda i: i
    bwd = lambda i: nstep - 1 - i
    tok = lambda idx: (lambda i: (0, idx(i), 0))
    feat = lambda idx: (lambda i: (0, 0, idx(i)))
    args = [gr, gr]
    specs = [pl.BlockSpec((bsz, N_GATES, span), feat(fwd)), pl.BlockSpec((bsz, N_GATES, span), feat(bwd))]
    m_shape = (2, bsz * N_HEADS, chunk)
    if has_init:
        args.append(m0)
        specs.append(_const_spec(m_shape))
    out_shape, out_specs = [], []
    if emit:
        out_shape += [jax.ShapeDtypeStruct((bsz, s, LANES), F32)] * 2
        out_specs += [pl.BlockSpec((bsz, span, LANES), tok(fwd)), pl.BlockSpec((bsz, span, LANES), tok(bwd))]
        out_shape += [jax.ShapeDtypeStruct((bsz, N_HEADS, s), F32)] * 2
        out_specs += [pl.BlockSpec((bsz, N_HEADS, span), feat(fwd)), pl.BlockSpec((bsz, N_HEADS, span), feat(bwd))]
        out_shape += [jax.ShapeDtypeStruct((bsz, s, 2 * N_HEADS * LANES), BF16)] * 2
        out_specs += [pl.BlockSpec((bsz, span, 2 * N_HEADS * LANES), tok(fwd)),
                      pl.BlockSpec((bsz, span, 2 * N_HEADS * LANES), tok(bwd))]
    out_shape += [jax.ShapeDtypeStruct((bsz, N_HEADS, s), F32)] * 2
    out_specs += [pl.BlockSpec((bsz, N_HEADS, span), feat(fwd)), pl.BlockSpec((bsz, N_HEADS, span), feat(bwd))]
    out_shape.append(jax.ShapeDtypeStruct((nc, ngroup * N_HEADS, LANES), F32))
    out_specs.append(pl.BlockSpec((per, ngroup * N_HEADS, LANES), lambda i: (i, 0, 0)))
    if emit:
        scratch = [pltpu.VMEM(m_shape, F32)]
    else:
        out_shape.append(jax.ShapeDtypeStruct(m_shape, F32))
        out_specs.append(pl.BlockSpec(m_shape, lambda i: (0, 0, 0)))
        scratch = []
    kern = functools.partial(_mgate_kernel, chunk=chunk, per=per, nb=bsz, emit=emit, has_init=has_init)
    return pl.pallas_call(
        kern,
        out_shape=out_shape,
        grid=(nstep,),
        in_specs=specs,
        out_specs=out_specs,
        scratch_shapes=scratch,
        compiler_params=_params(("arbitrary",)),
        name=name,
    )(*args)


def _fold_lanes(x):
    blocks = [x[:, c:c + LANES] for c in range(0, x.shape[1], LANES)]
    return functools.reduce(lambda a, b: a + b, blocks)


def _tile_lanes(x, width):
    return jnp.concatenate([x] * (width // LANES), axis=-1)


def _mlstm_kernel(*refs, chunk, nb, emit, has_init):
    it = iter(refs)
    dec_ref = next(it)
    n_in = 7 if emit else 3
    dirs = [tuple(next(it) for _ in range(n_in)) for _ in range(2)]
    init = (next(it), next(it)) if has_init else None
    h_refs = (next(it), next(it)) if emit else None
    c_ref, n_ref = next(it), next(it)

    step = pl.program_id(0)
    nchain = 2 * nb * N_HEADS

    @pl.when(step == 0)
    def _():
        if has_init:
            c_ref[...] = init[0][...]
            n_ref[...] = init[1][...]
        else:
            c_ref[...] = jnp.zeros_like(c_ref)
            n_ref[...] = jnp.zeros_like(n_ref)

    row = lax.broadcasted_iota(jnp.int32, (chunk, chunk), 0)
    col = lax.broadcasted_iota(jnp.int32, (chunk, chunk), 1)
    visible = (row >= col, row <= col)

    for d, dir_refs in enumerate(dirs):
        if emit:
            q_ref, kt_ref, v_ref, col_ref, r_ref, rep_ref, ws_ref = dir_refs
        else:
            kt_ref, v_ref, ws_ref = dir_refs
        for b in range(nb):
            for hd in range(N_HEADS):
                j = _chain(d, b, hd, nb)
                lanes = slice(hd * HEAD_DIM, (hd + 1) * HEAD_DIM)
                decay = dec_ref[step * nchain + j]
                ws = ws_ref[b, hd:hd + 1, :].astype(BF16)
                kt = kt_ref[b, 0, lanes, :]
                v = v_ref[b, :, lanes]
                ct = c_ref[j]
                n_st = n_ref[j]
                if emit:
                    q = q_ref[b, :, lanes]
                    inter = rep_ref[b, :, hd * LANES:(hd + 1) * LANES].astype(F32)
                    ei = rep_ref[b, :, (N_HEADS + hd) * LANES:(N_HEADS + hd + 1) * LANES].astype(F32)
                    p = jnp.where(visible[d], jnp.exp(r_ref[b, hd:hd + 1, :] - col_ref[b, :, hd:hd + 1]), 0.0)
                    s_ts = _dot(q, kt) * p
                    num_intra = _dot(s_ts.astype(BF16), v)
                    num_inter = _dot(q, ct.astype(BF16))
                    den = jnp.sum(_fold_lanes(s_ts) + inter * _fold_lanes(q.astype(F32) * n_st),
                                  axis=-1, keepdims=True)
                    rden = 1.0 / jnp.maximum(jnp.abs(den), ei)
                    h = (num_intra + _tile_lanes(inter, HEAD_DIM) * num_inter) * _tile_lanes(rden, HEAD_DIM)
                    h_refs[d][b, :, lanes] = h.astype(BF16)
                c_ref[j] = decay * ct + _dot(kt * ws, v)
                n_ref[j] = decay * n_st + _dot_t(ws, kt)


def _mlstm_call(dec, q, kt, v, gate_f, gate_b, init, *, chunk, emit, name):
    bsz, nc, d, _ = kt.shape
    s = nc * chunk
    nchain = 2 * bsz * N_HEADS
    has_init = init is not None

    def dir_args(gate, idx):
        tok = lambda i: (0, idx(i), 0)
        feat = lambda i: (0, 0, idx(i))
        args, specs = [], []
        if emit:
            args.append(q)
            specs.append(pl.BlockSpec((bsz, chunk, d), tok))
        args += [kt, v]
        specs += [pl.BlockSpec((bsz, 1, d, chunk), lambda i: (0, idx(i), 0, 0)),
                  pl.BlockSpec((bsz, chunk, d), tok)]
        if emit:
            cols, r, rep, ws = gate
            args += [cols, r, rep]
            specs += [pl.BlockSpec((bsz, chunk, LANES), tok),
                      pl.BlockSpec((bsz, N_HEADS, chunk), feat),
                      pl.BlockSpec((bsz, chunk, 2 * N_HEADS * LANES), tok)]
        else:
            (ws,) = gate
        args.append(ws)
        specs.append(pl.BlockSpec((bsz, N_HEADS, chunk), feat))
        return args, specs

    fwd = lambda i: i
    bwd = lambda i: nc - 1 - i
    af, sf = dir_args(gate_f, fwd)
    ab, sb = dir_args(gate_b, bwd)
    args = [dec] + af + ab
    specs = [pl.BlockSpec(memory_space=pltpu.SMEM)] + sf + sb
    state_shapes = [(nchain, HEAD_DIM, HEAD_DIM), (nchain, 1, HEAD_DIM)]
    if has_init:
        args += list(init)
        specs += [_const_spec(sh) for sh in state_shapes]
    kern = functools.partial(_mlstm_kernel, chunk=chunk, nb=bsz, emit=emit, has_init=has_init)
    if emit:
        out_shape = [jax.ShapeDtypeStruct((bsz, s, d), BF16)] * 2
        out_specs = [pl.BlockSpec((bsz, chunk, d), lambda i: (0, fwd(i), 0)),
                     pl.BlockSpec((bsz, chunk, d), lambda i: (0, bwd(i), 0))]
        scratch = [pltpu.VMEM(sh, F32) for sh in state_shapes]
    else:
        out_shape = [jax.ShapeDtypeStruct(sh, F32) for sh in state_shapes]
        out_specs = [pl.BlockSpec(sh, lambda i: (0, 0, 0)) for sh in state_shapes]
        scratch = []
    return pl.pallas_call(
        kern,
        out_shape=out_shape,
        grid=(nc,),
        in_specs=specs,
        out_specs=out_specs,
        scratch_shapes=scratch,
        compiler_params=_params(("arbitrary",)),
        name=name,
    )(*args)


def _tail_kernel(hf_ref, hb_ref, o_ref, u_ref, vs_ref, ga_ref, gb_ref, h_ref, mods_ref, mods_ffn_ref,
                 ghead_ref, gsgu_ref, ws_ref, bs_ref, wa_ref, wb_ref, wo_ref,
                 g2_ref, win_ref, wout_ref, gfin_ref, out_ref, yb_ref, h2_ref, *, tm):
    @pl.when(pl.program_id(0) == 0)
    def _():
        h2_ref[...] = jnp.zeros_like(h2_ref)

    h2_prev = h2_ref[...]
    out_ref[0] = _rms(_ffn_body(h2_prev, mods_ffn_ref[0], 6, g2_ref[...], win_ref, wout_ref), gfin_ref[...])

    m = mods_ref[0]
    hm = hf_ref[0].astype(F32) + hb_ref[0].astype(F32)
    parts = []
    for hd in range(N_HEADS):
        x = hm[:, hd * HEAD_DIM:(hd + 1) * HEAD_DIM]
        xc = x - jnp.mean(x, axis=-1, keepdims=True)
        parts.append(xc * lax.rsqrt(jnp.mean(xc * xc, axis=-1, keepdims=True) + EPS))
    ln = jnp.concatenate(parts, axis=-1) * ghead_ref[...]
    y_a = (_sigmoid(o_ref[0].astype(F32)) * ln).astype(BF16)

    vn = _rms(_gelu_tanh(vs_ref[0].astype(F32)), gsgu_ref[...]).astype(BF16)
    for r in range(tm // SGU_CHUNK):
        rows = slice(r * SGU_CHUNK, (r + 1) * SGU_CHUNK)
        for gidx in range(N_GROUPS):
            cols = slice(gidx * GROUP_DIM, (gidx + 1) * GROUP_DIM)
            mixed = _dot(ws_ref[gidx], vn[rows, cols]) + bs_ref[:, gidx:gidx + 1]
            gu = _gelu_tanh(u_ref[0, rows, cols].astype(F32))
            yb_ref[rows, cols] = (gu * mixed).astype(BF16)

    mixed_out = (_sigmoid(ga_ref[0].astype(F32)) * _dot(y_a, wa_ref[...])
                 + _sigmoid(gb_ref[0].astype(F32)) * _dot(yb_ref[...], wb_ref[...]))
    y = _dot(mixed_out.astype(BF16), wo_ref[...])
    h2_ref[...] = h_ref[0] + m[5:6] * y


def _tail_call(hf, hb, o, u, vs, ga, gb, h, mods, g_head, g_sgu, w_s, b_s_t, w_a, w_b, w_o,
               g2, w_in, w_out, g_final, *, tm):
    bsz, s, d = h.shape
    ntile = bsz * (s // tm)
    merge_tile = lambda t: jnp.minimum(t, ntile - 1)
    ffn_tile = lambda t: jnp.maximum(t - 1, 0)
    tok = pl.BlockSpec((1, tm, d), lambda t: (merge_tile(t) % bsz, merge_tile(t) // bsz, 0))
    vec = _const_spec((1, d))
    specs = [tok] * 8 + [
        pl.BlockSpec((1, N_MOD, d), lambda t: (merge_tile(t) % bsz, 0, 0)),
        pl.BlockSpec((1, N_MOD, d), lambda t: (ffn_tile(t) % bsz, 0, 0)),
        vec, vec, _const_spec(w_s.shape), _const_spec(b_s_t.shape),
        _const_spec(w_a.shape), _const_spec(w_b.shape), _const_spec(w_o.shape),
        vec, _const_spec(w_in.shape), _const_spec(w_out.shape), vec,
    ]
    return pl.pallas_call(
        functools.partial(_tail_kernel, tm=tm),
        out_shape=jax.ShapeDtypeStruct((bsz, s, d), F32),
        grid=(ntile + 1,),
        in_specs=specs,
        out_specs=pl.BlockSpec((1, tm, d), lambda t: (ffn_tile(t) % bsz, ffn_tile(t) // bsz, 0)),
        scratch_shapes=[pltpu.VMEM((tm, d), BF16), pltpu.VMEM((tm, d), F32)],
        compiler_params=_params(("arbitrary",)),
        name="tail",
    )(hf, hb, o, u, vs, ga, gb, h, mods, mods, g_head.reshape(1, d), g_sgu.reshape(1, d),
      w_s, b_s_t, w_a, w_b, w_o, g2.reshape(1, d), w_in, w_out, g_final.reshape(1, d))


def _pos_tables(rows):
    quarter = D_MODEL // 4
    freqs = jnp.exp(-math.log(POS_BASE) * jnp.arange(quarter, dtype=F32) / quarter)
    ar = jnp.arange(rows, dtype=F32)[:, None] * freqs
    ac = jnp.arange(GRID_W, dtype=F32)[:, None] * freqs
    return (jnp.concatenate([jnp.sin(ar), jnp.cos(ar)], axis=-1),
            jnp.concatenate([jnp.sin(ac), jnp.cos(ac)], axis=-1))


def kernel(x, c, ctx, c_ctx, w_ada, b_ada, g_ffn1, w_ffn1_in, w_ffn1_out, g_mix, w_in, b_gates,
           conv_qk_w, conv_qk_b, g_head, g_sgu, w_s, b_s, w_branch_a, w_branch_b, w_out,
           g_ffn2, w_ffn2_in, w_ffn2_out, g_final):
    bsz, seq, d = x.shape
    ctx_len = ctx.shape[1]
    layer = 0
    pos = _pos_tables(seq // GRID_W)

    cond = jnp.concatenate([c, c_ctx[None], jnp.zeros((HALO - bsz - 1, d), F32)], axis=0)
    mods = _mods_call(cond, w_ada[layer], b_ada[layer]).reshape(HALO, N_MOD, d)
    lat_row = lambda b: b
    ctx_row = lambda b: bsz

    w1_in = w_ffn1_in[layer].astype(BF16)
    w1_out = w_ffn1_out[layer].astype(BF16)
    wi_t = jnp.swapaxes(w_in[layer], 0, 1)
    wgt = wi_t[3 * d:3 * d + N_GATES].astype(BF16)
    bg = b_gates[layer]
    nstep = (seq // FFN_TM) * bsz
    n_rest = wi_t.shape[0] - 3 * d - N_GATES
    casts = [
        (wi_t, 0, 2 * d, 2 * d // nstep),
        (wi_t, 2 * d, d, d // nstep),
        (wi_t, 3 * d + N_GATES, n_rest, n_rest // nstep),
        (w_ffn2_in[layer], 0, d, d // nstep),
        (w_ffn2_out[layer], 0, D_FF, 2 * D_FF // nstep),
        (w_branch_a[layer], 0, d, d // nstep),
        (w_branch_b[layer], 0, d, d // nstep),
        (w_out[layer], 0, d, d // nstep),
    ]

    def bg_row(tm):
        return jnp.broadcast_to(bg[:, None], (N_GATES, tm))

    def dec_table(dec):
        return dec[:, :, 0].reshape(-1)

    h, wqk, wv, wrest, w2_in, w2_out, wa, wb, wo = _ffn_call(
        x, pos, mods, lat_row, g_ffn1[layer], w1_in, w1_out, casts, mod_base=0, tm=FFN_TM, name="ffn1")
    (hc,) = _ffn_call(ctx, None, mods, ctx_row, g_ffn1[layer], w1_in, w1_out, [],
                      mod_base=0, tm=ctx_len, name="ffn1_ctx")

    _, kt_c, v_c, gr_c = _proj_call(
        hc, mods, ctx_row, g_mix[layer], wqk, wv, wgt, None, conv_qk_w[layer], conv_qk_b[layer],
        bg_row(ctx_len), tm=ctx_len, name="proj_ctx")
    q_l, kt_l, v_l, gr_l, o_l, u_l, vs_l, ga_l, gb_l = _proj_call(
        h, mods, lat_row, g_mix[layer], wqk, wv, wgt, wrest, conv_qk_w[layer], conv_qk_b[layer],
        bg_row(PROJ_TM), tm=PROJ_TM, name="proj")

    ws_cf, ws_cb, dec_c, m_c = _mgate_call(gr_c, None, chunk=MLSTM_L, per=ctx_len // MLSTM_L, emit=False,
                                           name="mgate_ctx")
    col_f, col_b, r_f, r_b, rep_f, rep_b, ws_f, ws_b, dec_l = _mgate_call(
        gr_l, m_c, chunk=MLSTM_L, per=MGATE_PER, emit=True, name="mgate")
    state = _mlstm_call(dec_table(dec_c), None, kt_c, v_c, (ws_cf,), (ws_cb,), None,
                        chunk=MLSTM_L, emit=False, name="mlstm_ctx")
    hf, hb = _mlstm_call(dec_table(dec_l), q_l, kt_l, v_l, (col_f, r_f, rep_f, ws_f), (col_b, r_b, rep_b, ws_b),
                         state, chunk=MLSTM_L, emit=True, name="mlstm")

    return _tail_call(hf, hb, o_l, u_l, vs_l, ga_l, gb_l, h, mods, g_head[layer], g_sgu[layer],
                      w_s[layer].astype(BF16), b_s[layer].T, wa, wb, wo,
                      g_ffn2[layer], w2_in, w2_out, g_final, tm=TAIL_TM)
```

```python
import functools
import math

import jax
import jax.numpy as jnp
from jax import lax
from jax.experimental import pallas as pl
from jax.experimental.pallas import tpu as pltpu

F32 = jnp.float32
BF16 = jnp.bfloat16

D_MODEL = 1024
GRID_W = 64
N_HEADS = 4
HEAD_DIM = 256
N_GROUPS = 4
GROUP_DIM = 256
SGU_CHUNK = 128
D_FF = 2816
N_MOD = 9
N_GATES = 16
POS_BASE = 10000.0
EPS = 1e-6

LANES = 128
HALO = 8
VMEM_LIMIT = 56 * 1024 * 1024
MASK_NEG = -1e30

FFN_TM = 512
PROJ_TM = 512
TAIL_TM = 256
MLSTM_L = 256
MGATE_PER = 4
FF_CHUNKS = ((0, 1024), (1024, 2048), (2048, D_FF))


def _dot(a, b):
    return jnp.dot(a, b, preferred_element_type=F32)


def _dot_t(a, b_t):
    return lax.dot_general(a, b_t, (((1,), (1,)), ((), ())), preferred_element_type=F32)


def _sigmoid(x):
    return 1.0 / (1.0 + jnp.exp2(x * -math.log2(math.e)))


def _gelu_tanh(x):
    k = -2.0 * math.sqrt(2.0 / math.pi) * math.log2(math.e)
    return x / (1.0 + jnp.exp2(x * (k + (0.044715 * k) * (x * x))))


def _log_sigmoid(x):
    return jnp.minimum(x, 0.0) - jnp.log(1.0 + jnp.exp(-jnp.abs(x)))


def _rms(x, g):
    return x * lax.rsqrt(jnp.mean(x * x, axis=-1, keepdims=True) + EPS) * g


def _norm_mod(x, g, shift, scale):
    return _rms(x, g) * (1.0 + scale) + shift


def _const_spec(shape):
    zeros = (0,) * len(shape)
    return pl.BlockSpec(shape, lambda *_: zeros, pipeline_mode=pl.Buffered(1))


def _params(sem):
    return pltpu.CompilerParams(dimension_semantics=sem, vmem_limit_bytes=VMEM_LIMIT)


def _mods_kernel(c_ref, w_ref, b_ref, o_ref):
    c = c_ref[...]
    s = (c * _sigmoid(c)).astype(BF16)
    o_ref[...] = _dot(s, w_ref[...].astype(BF16)) + b_ref[...]


def _mods_call(cond, w_ada, b_ada):
    rows, d = cond.shape
    n = w_ada.shape[1]
    tn = 1024
    return pl.pallas_call(
        _mods_kernel,
        out_shape=jax.ShapeDtypeStruct((rows, n), F32),
        grid=(n // tn,),
        in_specs=[
            pl.BlockSpec((rows, d), lambda j: (0, 0)),
            pl.BlockSpec((d, tn), lambda j: (0, j)),
            pl.BlockSpec((1, tn), lambda j: (0, j)),
        ],
        out_specs=pl.BlockSpec((rows, tn), lambda j: (0, j)),
        compiler_params=_params(("arbitrary",)),
        name="mods",
    )(cond, w_ada, b_ada.reshape(1, n))


def _ffn_body(x, m, mod_base, g, win_ref, wout_ref):
    shift = m[mod_base:mod_base + 1]
    scale = m[mod_base + 1:mod_base + 2]
    gate = m[mod_base + 2:mod_base + 3]
    hb = _norm_mod(x, g, shift, scale).astype(BF16)
    y = None
    for c0, c1 in FF_CHUNKS:
        a = _dot(hb, win_ref[:, c0:c1])
        b = _dot(hb, win_ref[:, D_FF + c0:D_FF + c1])
        act = (a * _sigmoid(a) * b).astype(BF16)
        part = _dot(act, wout_ref[c0:c1, :])
        y = part if y is None else y + part
    return x + 0.5 * gate * y


def _ffn_kernel(*refs, tm, mod_base, add_pos, n_cast):
    it = iter(refs)
    x_ref = next(it)
    prow_ref, pcol_ref = (next(it), next(it)) if add_pos else (None, None)
    mods_ref, g_ref, win_ref, wout_ref = (next(it) for _ in range(4))
    cast_in = [next(it) for _ in range(n_cast)]
    o_ref = next(it)
    cast_out = [next(it) for _ in range(n_cast)]

    x = x_ref[0]
    if add_pos:
        pcol = pcol_ref[...]
        half = pcol.shape[1]
        pos = [jnp.concatenate([jnp.broadcast_to(prow_ref[r:r + 1, :], (GRID_W, half)), pcol], axis=-1)
               for r in range(tm // GRID_W)]
        x = x + jnp.concatenate(pos, axis=0)
    o_ref[0] = _ffn_body(x, mods_ref[0], mod_base, g_ref[...], win_ref, wout_ref)

    for src, dst in zip(cast_in, cast_out):
        dst[...] = src[...].astype(BF16)


def _ffn_call(x, pos, mods, mod_row, g, w_in, w_out, casts, *, mod_base, tm, name):
    bsz, s, d = x.shape
    add_pos = pos is not None
    nstep = (s // tm) * bsz
    args = [x]
    specs = [pl.BlockSpec((1, tm, d), lambda i, b: (b, i, 0))]
    if add_pos:
        prow, pcol = pos
        args += [prow, pcol]
        specs += [pl.BlockSpec((tm // GRID_W, prow.shape[1]), lambda i, b: (i, 0)),
                  _const_spec(pcol.shape)]
    args += [mods, g.reshape(1, d), w_in, w_out]
    specs += [
        pl.BlockSpec((1, N_MOD, d), lambda i, b: (mod_row(b), 0, 0)),
        _const_spec((1, d)),
        _const_spec(w_in.shape),
        _const_spec(w_out.shape),
    ]
    out_shape = [jax.ShapeDtypeStruct((bsz, s, d), F32)]
    out_specs = [pl.BlockSpec((1, tm, d), lambda i, b: (b, i, 0))]
    for w, first, count, rows in casts:
        nblk = count // rows
        assert nblk * rows == count and nblk <= nstep
        blk = lambda i, b, nblk=nblk: jnp.minimum(i * bsz + b, nblk - 1)
        args.append(w)
        specs.append(pl.BlockSpec((pl.Element(rows), pl.Element(w.shape[1])),
                                  lambda i, b, blk=blk, first=first, rows=rows:
                                  (pl.multiple_of(first + blk(i, b) * rows, HALO), 0)))
        out_shape.append(jax.ShapeDtypeStruct((count, w.shape[1]), BF16))
        out_specs.append(pl.BlockSpec((rows, w.shape[1]), lambda i, b, blk=blk: (blk(i, b), 0)))
    kern = functools.partial(_ffn_kernel, tm=tm, mod_base=mod_base, add_pos=add_pos, n_cast=len(casts))
    return pl.pallas_call(
        kern,
        out_shape=out_shape,
        grid=(s // tm, bsz),
        in_specs=specs,
        out_specs=out_specs,
        compiler_params=_params(("arbitrary", "arbitrary")),
        name=name,
    )(*args)


def _proj_kernel(*refs, tm, with_rest):
    it = iter(refs)
    h_ref, hp_ref, hx_ref, mods_ref, g_ref = (next(it) for _ in range(5))
    wqk_ref, wv_ref, wgt_ref = (next(it) for _ in range(3))
    wrest_ref = next(it) if with_rest else None
    cw_ref, cb_ref, bgr_ref = (next(it) for _ in range(3))
    q_ref, kt_ref, v_ref, gr_ref = (next(it) for _ in range(4))
    rest_refs = [next(it) for _ in range(5)] if with_rest else []
    raw_ref = next(it)

    i = pl.program_id(0)
    nt = pl.num_programs(0)
    m = mods_ref[0]
    shift, scale = m[3:4], m[4:5]
    g = g_ref[...]
    hn = _norm_mod(h_ref[0], g, shift, scale)
    prev_ok = (i > 0).astype(F32)
    next_ok = (i < nt - 1).astype(F32)
    hn_prev = _norm_mod(hp_ref[0], g, shift, scale) * prev_ok
    hn_next = _norm_mod(hx_ref[0], g, shift, scale) * next_ok
    hb = hn.astype(BF16)
    ext = jnp.concatenate([hn_prev, hn, hn_next], axis=0).astype(BF16)

    raw_ref[...] = _dot_t(ext, wqk_ref[...])
    sub8 = lax.broadcasted_iota(jnp.int32, (HALO, HEAD_DIM), 0)

    def conv_chunk(c):
        cols = slice(c * HEAD_DIM, (c + 1) * HEAD_DIM)
        cw = cw_ref[:, cols]
        x0 = raw_ref[HALO:HALO + tm, cols]
        down = pltpu.roll(x0, 1, axis=0)
        first = jnp.where(sub8 == 0, raw_ref[HALO - 1:HALO, cols], down[:HALO])
        xm = jnp.concatenate([first, down[HALO:]], axis=0)
        up = pltpu.roll(x0, tm - 1, axis=0)
        last = jnp.where(sub8 == HALO - 1, raw_ref[HALO + tm:HALO + tm + 1, cols], up[tm - HALO:])
        xp = jnp.concatenate([up[:tm - HALO], last], axis=0)
        acc = xm * cw[0:1] + x0 * cw[1:2] + xp * cw[2:3] + cb_ref[:, cols]
        act = acc * _sigmoid(acc)
        if c < N_HEADS:
            q_ref[0, :, cols] = (act * HEAD_DIM ** -0.5).astype(BF16)
        else:
            kc = slice((c - N_HEADS) * HEAD_DIM, (c - N_HEADS + 1) * HEAD_DIM)
            act_t = act.T.astype(BF16)
            for j in range(tm // MLSTM_L):
                kt_ref[0, j, kc, :] = act_t[:, j * MLSTM_L:(j + 1) * MLSTM_L]

    def rest_piece(j):
        rest_refs[j][0] = _dot_t(hb, wrest_ref[j * D_MODEL:(j + 1) * D_MODEL, :]).astype(BF16)

    grow = _dot_t(wgt_ref[...], hb) + bgr_ref[...]
    sub = lax.broadcasted_iota(jnp.int32, grow.shape, 0)
    gr_ref[0] = jnp.where(sub % 8 >= N_HEADS, _log_sigmoid(grow), grow)

    v_ref[0] = _dot_t(hb, wv_ref[...]).astype(BF16)
    for c in range(2 * N_HEADS):
        if with_rest and c < 5:
            rest_piece(c)
        conv_chunk(c)


def _proj_call(h, mods, mod_row, g, wqk, wv, wgt, wrest, conv_w, conv_b, bg_row, *, tm, name):
    bsz, s, d = h.shape
    with_rest = wrest is not None
    nhb = tm // HALO
    last = s // HALO - 1
    args = [h, h, h, mods, g.reshape(1, d), wqk, wv, wgt]
    specs = [
        pl.BlockSpec((1, tm, d), lambda i, b: (b, i, 0)),
        pl.BlockSpec((1, HALO, d), lambda i, b: (b, jnp.maximum(i * nhb - 1, 0), 0)),
        pl.BlockSpec((1, HALO, d), lambda i, b: (b, jnp.minimum((i + 1) * nhb, last), 0)),
        pl.BlockSpec((1, N_MOD, d), lambda i, b: (mod_row(b), 0, 0)),
        _const_spec((1, d)),
        _const_spec(wqk.shape), _const_spec(wv.shape), _const_spec(wgt.shape),
    ]
    if with_rest:
        args.append(wrest)
        specs.append(_const_spec(wrest.shape))
    args += [conv_w, conv_b.reshape(1, -1), bg_row]
    specs += [_const_spec(conv_w.shape), _const_spec((1, conv_b.shape[0])), _const_spec(bg_row.shape)]
    tok = lambda i, b: (b, i, 0)
    out_shape = [
        jax.ShapeDtypeStruct((bsz, s, d), BF16),
        jax.ShapeDtypeStruct((bsz, s // MLSTM_L, d, MLSTM_L), BF16),
        jax.ShapeDtypeStruct((bsz, s, d), BF16),
        jax.ShapeDtypeStruct((bsz, N_GATES, s), F32),
    ]
    out_specs = [
        pl.BlockSpec((1, tm, d), tok),
        pl.BlockSpec((1, tm // MLSTM_L, d, MLSTM_L), lambda i, b: (b, i, 0, 0)),
        pl.BlockSpec((1, tm, d), tok),
        pl.BlockSpec((1, N_GATES, tm), lambda i, b: (b, 0, i)),
    ]
    if with_rest:
        out_shape += [jax.ShapeDtypeStruct((bsz, s, d), BF16)] * 5
        out_specs += [pl.BlockSpec((1, tm, d), tok)] * 5
    kern = functools.partial(_proj_kernel, tm=tm, with_rest=with_rest)
    return pl.pallas_call(
        kern,
        out_shape=out_shape,
        grid=(s // tm, bsz),
        in_specs=specs,
        out_specs=out_specs,
        scratch_shapes=[pltpu.VMEM((tm + 2 * HALO, 2 * d), F32)],
        compiler_params=_params(("arbitrary", "arbitrary")),
        name=name,
    )(*args)


def _split3(x):
    hi = x.astype(BF16)
    r1 = x - hi.astype(F32)
    mid = r1.astype(BF16)
    lo = (r1 - mid.astype(F32)).astype(BF16)
    return hi, mid, lo


def _chain(d, b, hd, nb):
    return (d * nb + b) * N_HEADS + hd


def _mgate_kernel(*refs, chunk, per, nb, emit, has_init):
    it = iter(refs)
    gr_refs = (next(it), next(it))
    m0_ref = next(it) if has_init else None
    col_refs = (next(it), next(it)) if emit else None
    r_refs = (next(it), next(it)) if emit else None
    rep_refs = (next(it), next(it)) if emit else None
    ws_refs = (next(it), next(it))
    dec_ref = next(it)
    m_ref = next(it)

    @pl.when(pl.program_id(0) == 0)
    def _():
        m_ref[...] = m0_ref[...] if has_init else jnp.zeros_like(m_ref)

    row = lax.broadcasted_iota(jnp.int32, (chunk, chunk), 0)
    col = lax.broadcasted_iota(jnp.int32, (chunk, chunk), 1)
    lane = lax.broadcasted_iota(jnp.int32, (nb * N_HEADS, chunk), 1)
    ones = jnp.ones((chunk, chunk), BF16)
    sel_r = lax.broadcasted_iota(jnp.int32, (LANES, 2 * N_HEADS * LANES), 0)
    sel_j = lax.broadcasted_iota(jnp.int32, (LANES, 2 * N_HEADS * LANES), 1) // LANES
    spread = jnp.where(sel_r == sel_j + N_HEADS, 1.0, 0.0).astype(BF16)

    tri_ones = [jnp.concatenate([jnp.where(keep, 1.0, 0.0).astype(BF16), ones], axis=1)
                for keep in (row <= col, row >= col)]

    units = [(d, sub) for d in range(2) for sub in range(per)]

    def lanes_of(d, sub):
        first = sub * chunk if d == 0 else (per - 1 - sub) * chunk
        return slice(first, first + chunk)

    li, bq, g4 = {}, {}, {}
    for d, sub in units:
        parts = []
        for b in range(nb):
            gr = gr_refs[d][b, :, lanes_of(d, sub)]
            sums = _dot(jnp.concatenate(_split3(gr), axis=0), tri_ones[d])
            sums = sums[:N_GATES] + sums[N_GATES:2 * N_GATES] + sums[2 * N_GATES:]
            b_all = sums[:, :chunk]
            g_all = sums[:, chunk:]
            f_rows = slice(8 * d + N_HEADS, 8 * d + 2 * N_HEADS)
            parts.append((gr[8 * d:8 * d + N_HEADS], b_all[f_rows], g_all[f_rows]))
        li[d, sub] = jnp.concatenate([p[0] for p in parts], axis=0)
        bq[d, sub] = jnp.concatenate([p[1] for p in parts], axis=0)
        g4[d, sub] = jnp.concatenate([p[2] for p in parts], axis=0)

    r = {u: li[u] - bq[u] for u in units}
    m_st = {}
    for d in range(2):
        m = m_ref[d]
        for sub in range(per):
            u = (d, sub)
            m_st[u] = m
            m = g4[u] + jnp.maximum(m, jnp.max(r[u], axis=-1, keepdims=True))
            ws = jnp.exp(g4[u] + r[u] - m)
            for b in range(nb):
                ws_refs[d][b, :, lanes_of(d, sub)] = ws[b * N_HEADS:(b + 1) * N_HEADS]
            dec_ref[sub, d * nb * N_HEADS:(d + 1) * nb * N_HEADS, :] = jnp.exp(g4[u] + m_st[u] - m)[:, :LANES]
        m_ref[d] = m
    if not emit:
        return

    run = dict(r)
    k = 1
    while k < chunk:
        for u in units:
            if u[0] == 0:
                shifted = jnp.where(lane >= k, pltpu.roll(run[u], k, axis=1), MASK_NEG)
            else:
                shifted = jnp.where(lane < chunk - k, pltpu.roll(run[u], chunk - k, axis=1), MASK_NEG)
            run[u] = jnp.maximum(run[u], shifted)
        k *= 2

    cols = {}
    for u in units:
        m4 = jnp.maximum(m_st[u], run[u])
        inter = jnp.exp(m_st[u] - m4)
        ei = jnp.exp(-bq[u] - m4)
        for b in range(nb):
            sl = slice(b * N_HEADS, (b + 1) * N_HEADS)
            rows = jnp.concatenate([m4[sl], inter[sl], ei[sl],
                                    jnp.zeros((LANES - 3 * N_HEADS, chunk), F32)], axis=0)
            cols[u, b] = rows.T
    for ((d, sub), b), c in cols.items():
        rep_refs[d][b, lanes_of(d, sub), :] = _dot(c.astype(BF16), spread).astype(BF16)
    for ((d, sub), b), c in cols.items():
        col_refs[d][b, lanes_of(d, sub), :] = c
        r_refs[d][b, :, lanes_of(d, sub)] = r[d, sub][b * N_HEADS:(b + 1) * N_HEADS]


def _mgate_call(gr, m0, *, chunk, per, emit, name):
    bsz, _, s = gr.shape
    nc = s // chunk
    nstep = nc // per
    assert nstep * per == nc
    span = per * chunk
    ngroup = 2 * bsz
    has_init = m0 is not None
    fwd = lambda i: i
    bwd = lambda i: nstep - 1 - i
    tok = lambda idx: (lambda i: (0, idx(i), 0))
    feat = lambda idx: (lambda i: (0, 0, idx(i)))
    args = [gr, gr]
    specs = [pl.BlockSpec((bsz, N_GATES, span), feat(fwd)), pl.BlockSpec((bsz, N_GATES, span), feat(bwd))]
    m_shape = (2, bsz * N_HEADS, chunk)
    if has_init:
        args.append(m0)
        specs.append(_const_spec(m_shape))
    out_shape, out_specs = [], []
    if emit:
        out_shape += [jax.ShapeDtypeStruct((bsz, s, LANES), F32)] * 2
        out_specs += [pl.BlockSpec((bsz, span, LANES), tok(fwd)), pl.BlockSpec((bsz, span, LANES), tok(bwd))]
        out_shape += [jax.ShapeDtypeStruct((bsz, N_HEADS, s), F32)] * 2
        out_specs += [pl.BlockSpec((bsz, N_HEADS, span), feat(fwd)), pl.BlockSpec((bsz, N_HEADS, span), feat(bwd))]
        out_shape += [jax.ShapeDtypeStruct((bsz, s, 2 * N_HEADS * LANES), BF16)] * 2
        out_specs += [pl.BlockSpec((bsz, span, 2 * N_HEADS * LANES), tok(fwd)),
                      pl.BlockSpec((bsz, span, 2 * N_HEADS * LANES), tok(bwd))]
    out_shape += [jax.ShapeDtypeStruct((bsz, N_HEADS, s), F32)] * 2
    out_specs += [pl.BlockSpec((bsz, N_HEADS, span), feat(fwd)), pl.BlockSpec((bsz, N_HEADS, span), feat(bwd))]
    out_shape.append(jax.ShapeDtypeStruct((nc, ngroup * N_HEADS, LANES), F32))
    out_specs.append(pl.BlockSpec((per, ngroup * N_HEADS, LANES), lambda i: (i, 0, 0)))
    if emit:
        scratch = [pltpu.VMEM(m_shape, F32)]
    else:
        out_shape.append(jax.ShapeDtypeStruct(m_shape, F32))
        out_specs.append(pl.BlockSpec(m_shape, lambda i: (0, 0, 0)))
        scratch = []
    kern = functools.partial(_mgate_kernel, chunk=chunk, per=per, nb=bsz, emit=emit, has_init=has_init)
    return pl.pallas_call(
        kern,
        out_shape=out_shape,
        grid=(nstep,),
        in_specs=specs,
        out_specs=out_specs,
        scratch_shapes=scratch,
        compiler_params=_params(("arbitrary",)),
        name=name,
    )(*args)


def _fold_lanes(x):
    blocks = [x[:, c:c + LANES] for c in range(0, x.shape[1], LANES)]
    return functools.reduce(lambda a, b: a + b, blocks)


def _tile_lanes(x, width):
    return jnp.concatenate([x] * (width // LANES), axis=-1)


def _mlstm_kernel(*refs, chunk, nb, emit, has_init):
    it = iter(refs)
    dec_ref = next(it)
    n_in = 7 if emit else 3
    dirs = [tuple(next(it) for _ in range(n_in)) for _ in range(2)]
    init = (next(it), next(it)) if has_init else None
    h_refs = (next(it), next(it)) if emit else None
    c_ref, n_ref = next(it), next(it)

    step = pl.program_id(0)
    nchain = 2 * nb * N_HEADS

    @pl.when(step == 0)
    def _():
        if has_init:
            c_ref[...] = init[0][...]
            n_ref[...] = init[1][...]
        else:
            c_ref[...] = jnp.zeros_like(c_ref)
            n_ref[...] = jnp.zeros_like(n_ref)

    row = lax.broadcasted_iota(jnp.int32, (chunk, chunk), 0)
    col = lax.broadcasted_iota(jnp.int32, (chunk, chunk), 1)
    visible = (row >= col, row <= col)

    for d, dir_refs in enumerate(dirs):
        if emit:
            q_ref, kt_ref, v_ref, col_ref, r_ref, rep_ref, ws_ref = dir_refs
        else:
            kt_ref, v_ref, ws_ref = dir_refs
        for b in range(nb):
            for hd in range(N_HEADS):
                j = _chain(d, b, hd, nb)
                lanes = slice(hd * HEAD_DIM, (hd + 1) * HEAD_DIM)
                decay = dec_ref[step * nchain + j]
                ws = ws_ref[b, hd:hd + 1, :].astype(BF16)
                kt = kt_ref[b, 0, lanes, :]
                v = v_ref[b, :, lanes]
                ct = c_ref[j]
                n_st = n_ref[j]
                if emit:
                    q = q_ref[b, :, lanes]
                    inter = rep_ref[b, :, hd * LANES:(hd + 1) * LANES].astype(F32)
                    ei = rep_ref[b, :, (N_HEADS + hd) * LANES:(N_HEADS + hd + 1) * LANES].astype(F32)
                    p = jnp.where(visible[d], jnp.exp(r_ref[b, hd:hd + 1, :] - col_ref[b, :, hd:hd + 1]), 0.0)
                    s_ts = _dot(q, kt) * p
                    num_intra = _dot(s_ts.astype(BF16), v)
                    num_inter = _dot(q, ct.astype(BF16))
                    den = jnp.sum(_fold_lanes(s_ts) + inter * _fold_lanes(q.astype(F32) * n_st),
                                  axis=-1, keepdims=True)
                    rden = 1.0 / jnp.maximum(jnp.abs(den), ei)
                    h = (num_intra + _tile_lanes(inter, HEAD_DIM) * num_inter) * _tile_lanes(rden, HEAD_DIM)
                    h_refs[d][b, :, lanes] = h.astype(BF16)
                c_ref[j] = decay * ct + _dot(kt * ws, v)
                n_ref[j] = decay * n_st + _dot_t(ws, kt)


def _mlstm_call(dec, q, kt, v, gate_f, gate_b, init, *, chunk, emit, name):
    bsz, nc, d, _ = kt.shape
    s = nc * chunk
    nchain = 2 * bsz * N_HEADS
    has_init = init is not None

    def dir_args(gate, idx):
        tok = lambda i: (0, idx(i), 0)
        feat = lambda i: (0, 0, idx(i))
        args, specs = [], []
        if emit:
            args.append(q)
            specs.append(pl.BlockSpec((bsz, chunk, d), tok))
        args += [kt, v]
        specs += [pl.BlockSpec((bsz, 1, d, chunk), lambda i: (0, idx(i), 0, 0)),
                  pl.BlockSpec((bsz, chunk, d), tok)]
        if emit:
            cols, r, rep, ws = gate
            args += [cols, r, rep]
            specs += [pl.BlockSpec((bsz, chunk, LANES), tok),
                      pl.BlockSpec((bsz, N_HEADS, chunk), feat),
                      pl.BlockSpec((bsz, chunk, 2 * N_HEADS * LANES), tok)]
        else:
            (ws,) = gate
        args.append(ws)
        specs.append(pl.BlockSpec((bsz, N_HEADS, chunk), feat))
        return args, specs

    fwd = lambda i: i
    bwd = lambda i: nc - 1 - i
    af, sf = dir_args(gate_f, fwd)
    ab, sb = dir_args(gate_b, bwd)
    args = [dec] + af + ab
    specs = [pl.BlockSpec(memory_space=pltpu.SMEM)] + sf + sb
    state_shapes = [(nchain, HEAD_DIM, HEAD_DIM), (nchain, 1, HEAD_DIM)]
    if has_init:
        args += list(init)
        specs += [_const_spec(sh) for sh in state_shapes]
    kern = functools.partial(_mlstm_kernel, chunk=chunk, nb=bsz, emit=emit, has_init=has_init)
    if emit:
        out_shape = [jax.ShapeDtypeStruct((bsz, s, d), BF16)] * 2
        out_specs = [pl.BlockSpec((bsz, chunk, d), lambda i: (0, fwd(i), 0)),
                     pl.BlockSpec((bsz, chunk, d), lambda i: (0, bwd(i), 0))]
        scratch = [pltpu.VMEM(sh, F32) for sh in state_shapes]
    else:
        out_shape = [jax.ShapeDtypeStruct(sh, F32) for sh in state_shapes]
        out_specs = [pl.BlockSpec(sh, lambda i: (0, 0, 0)) for sh in state_shapes]
        scratch = []
    return pl.pallas_call(
        kern,
        out_shape=out_shape,
        grid=(nc,),
        in_specs=specs,
        out_specs=out_specs,
        scratch_shapes=scratch,
        compiler_params=_params(("arbitrary",)),
        name=name,
    )(*args)


def _tail_kernel(hf_ref, hb_ref, o_ref, u_ref, vs_ref, ga_ref, gb_ref, h_ref, mods_ref, mods_ffn_ref,
                 ghead_ref, gsgu_ref, ws_ref, bs_ref, wa_ref, wb_ref, wo_ref,
                 g2_ref, win_ref, wout_ref, gfin_ref, out_ref, yb_ref, h2_ref, *, tm):
    @pl.when(pl.program_id(0) == 0)
    def _():
        h2_ref[...] = jnp.zeros_like(h2_ref)

    h2_prev = h2_ref[...]
    out_ref[0] = _rms(_ffn_body(h2_prev, mods_ffn_ref[0], 6, g2_ref[...], win_ref, wout_ref), gfin_ref[...])

    m = mods_ref[0]
    hm = hf_ref[0].astype(F32) + hb_ref[0].astype(F32)
    parts = []
    for hd in range(N_HEADS):
        x = hm[:, hd * HEAD_DIM:(hd + 1) * HEAD_DIM]
        xc = x - jnp.mean(x, axis=-1, keepdims=True)
        parts.append(xc * lax.rsqrt(jnp.mean(xc * xc, axis=-1, keepdims=True) + EPS))
    ln = jnp.concatenate(parts, axis=-1) * ghead_ref[...]
    y_a = (_sigmoid(o_ref[0].astype(F32)) * ln).astype(BF16)

    vn = _rms(_gelu_tanh(vs_ref[0].astype(F32)), gsgu_ref[...]).astype(BF16)
    for r in range(tm // SGU_CHUNK):
        rows = slice(r * SGU_CHUNK, (r + 1) * SGU_CHUNK)
        for gidx in range(N_GROUPS):
            cols = slice(gidx * GROUP_DIM, (gidx + 1) * GROUP_DIM)
            mixed = _dot(ws_ref[gidx], vn[rows, cols]) + bs_ref[:, gidx:gidx + 1]
            gu = _gelu_tanh(u_ref[0, rows, cols].astype(F32))
            yb_ref[rows, cols] = (gu * mixed).astype(BF16)

    mixed_out = (_sigmoid(ga_ref[0].astype(F32)) * _dot(y_a, wa_ref[...])
                 + _sigmoid(gb_ref[0].astype(F32)) * _dot(yb_ref[...], wb_ref[...]))
    y = _dot(mixed_out.astype(BF16), wo_ref[...])
    h2_ref[...] = h_ref[0] + m[5:6] * y


def _tail_call(hf, hb, o, u, vs, ga, gb, h, mods, g_head, g_sgu, w_s, b_s_t, w_a, w_b, w_o,
               g2, w_in, w_out, g_final, *, tm):
    bsz, s, d = h.shape
    ntile = bsz * (s // tm)
    merge_tile = lambda t: jnp.minimum(t, ntile - 1)
    ffn_tile = lambda t: jnp.maximum(t - 1, 0)
    tok = pl.BlockSpec((1, tm, d), lambda t: (merge_tile(t) % bsz, merge_tile(t) // bsz, 0))
    vec = _const_spec((1, d))
    specs = [tok] * 8 + [
        pl.BlockSpec((1, N_MOD, d), lambda t: (merge_tile(t) % bsz, 0, 0)),
        pl.BlockSpec((1, N_MOD, d), lambda t: (ffn_tile(t) % bsz, 0, 0)),
        vec, vec, _const_spec(w_s.shape), _const_spec(b_s_t.shape),
        _const_spec(w_a.shape), _const_spec(w_b.shape), _const_spec(w_o.shape),
        vec, _const_spec(w_in.shape), _const_spec(w_out.shape), vec,
    ]
    return pl.pallas_call(
        functools.partial(_tail_kernel, tm=tm),
        out_shape=jax.ShapeDtypeStruct((bsz, s, d), F32),
        grid=(ntile + 1,),
        in_specs=specs,
        out_specs=pl.BlockSpec((1, tm, d), lambda t: (ffn_tile(t) % bsz, ffn_tile(t) // bsz, 0)),
        scratch_shapes=[pltpu.VMEM((tm, d), BF16), pltpu.VMEM((tm, d), F32)],
        compiler_params=_params(("arbitrary",)),
        name="tail",
    )(hf, hb, o, u, vs, ga, gb, h, mods, mods, g_head.reshape(1, d), g_sgu.reshape(1, d),
      w_s, b_s_t, w_a, w_b, w_o, g2.reshape(1, d), w_in, w_out, g_final.reshape(1, d))


def _pos_tables(rows):
    quarter = D_MODEL // 4
    freqs = jnp.exp(-math.log(POS_BASE) * jnp.arange(quarter, dtype=F32) / quarter)
    ar = jnp.arange(rows, dtype=F32)[:, None] * freqs
    ac = jnp.arange(GRID_W, dtype=F32)[:, None] * freqs
    return (jnp.concatenate([jnp.sin(ar), jnp.cos(ar)], axis=-1),
            jnp.concatenate([jnp.sin(ac), jnp.cos(ac)], axis=-1))


def kernel(x, c, ctx, c_ctx, w_ada, b_ada, g_ffn1, w_ffn1_in, w_ffn1_out, g_mix, w_in, b_gates,
           conv_qk_w, conv_qk_b, g_head, g_sgu, w_s, b_s, w_branch_a, w_branch_b, w_out,
           g_ffn2, w_ffn2_in, w_ffn2_out, g_final):
    bsz, seq, d = x.shape
    ctx_len = ctx.shape[1]
    layer = 0
    pos = _pos_tables(seq // GRID_W)

    cond = jnp.concatenate([c, c_ctx[None], jnp.zeros((HALO - bsz - 1, d), F32)], axis=0)
    mods = _mods_call(cond, w_ada[layer], b_ada[layer]).reshape(HALO, N_MOD, d)
    lat_row = lambda b: b
    ctx_row = lambda b: bsz

    w1_in = w_ffn1_in[layer].astype(BF16)
    w1_out = w_ffn1_out[layer].astype(BF16)
    wi_t = jnp.swapaxes(w_in[layer], 0, 1)
    wgt = wi_t[3 * d:3 * d + N_GATES].astype(BF16)
    bg = b_gates[layer]
    nstep = (seq // FFN_TM) * bsz
    n_rest = wi_t.shape[0] - 3 * d - N_GATES
    casts = [
        (wi_t, 0, 2 * d, 2 * d // nstep),
        (wi_t, 2 * d, d, d // nstep),
        (wi_t, 3 * d + N_GATES, n_rest, n_rest // nstep),
        (w_ffn2_in[layer], 0, d, d // nstep),
        (w_ffn2_out[layer], 0, D_FF, 2 * D_FF // nstep),
        (w_branch_a[layer], 0, d, d // nstep),
        (w_branch_b[layer], 0, d, d // nstep),
        (w_out[layer], 0, d, d // nstep),
    ]

    def bg_row(tm):
        return jnp.broadcast_to(bg[:, None], (N_GATES, tm))

    def dec_table(dec):
        return dec[:, :, 0].reshape(-1)

    h, wqk, wv, wrest, w2_in, w2_out, wa, wb, wo = _ffn_call(
        x, pos, mods, lat_row, g_ffn1[layer], w1_in, w1_out, casts, mod_base=0, tm=FFN_TM, name="ffn1")
    (hc,) = _ffn_call(ctx.reshape(1, bsz * ctx_len, d), None, mods, ctx_row, g_ffn1[layer], w1_in, w1_out, [],
                      mod_base=0, tm=bsz * ctx_len, name="ffn1_ctx")
    hc = hc.reshape(bsz, ctx_len, d)

    _, kt_c, v_c, gr_c = _proj_call(
        hc, mods, ctx_row, g_mix[layer], wqk, wv, wgt, None, conv_qk_w[layer], conv_qk_b[layer],
        bg_row(ctx_len), tm=ctx_len, name="proj_ctx")
    q_l, kt_l, v_l, gr_l, o_l, u_l, vs_l, ga_l, gb_l = _proj_call(
        h, mods, lat_row, g_mix[layer], wqk, wv, wgt, wrest, conv_qk_w[layer], conv_qk_b[layer],
        bg_row(PROJ_TM), tm=PROJ_TM, name="proj")

    ws_cf, ws_cb, dec_c, m_c = _mgate_call(gr_c, None, chunk=MLSTM_L, per=ctx_len // MLSTM_L, emit=False,
                                           name="mgate_ctx")
    col_f, col_b, r_f, r_b, rep_f, rep_b, ws_f, ws_b, dec_l = _mgate_call(
        gr_l, m_c, chunk=MLSTM_L, per=MGATE_PER, emit=True, name="mgate")
    state = _mlstm_call(dec_table(dec_c), None, kt_c, v_c, (ws_cf,), (ws_cb,), None,
                        chunk=MLSTM_L, emit=False, name="mlstm_ctx")
    hf, hb = _mlstm_call(dec_table(dec_l), q_l, kt_l, v_l, (col_f, r_f, rep_f, ws_f), (col_b, r_b, rep_b, ws_b),
                         state, chunk=MLSTM_L, emit=True, name="mlstm")

    return _tail_call(hf, hb, o_l, u_l, vs_l, ga_l, gb_l, h, mods, g_head[layer], g_sgu[layer],
                      w_s[layer].astype(BF16), b_s[layer].T, wa, wb, wo,
                      g_ffn2[layer], w2_in, w2_out, g_final, tm=TAIL_TM)
```
